```python
import math
import jax
import jax.numpy as jnp
from jax import lax
import numpy as np

D_MODEL = 4096
BATCH = 4
SEQ = 4096
DEPTH = 4

N_MIXERS = 4
N_HEADS = 32
HEAD_DIM = D_MODEL // N_HEADS
D_FF = 4 * D_MODEL
ROPE_THETA = 10000.0
DILATED_PATTERNS = ((128, 1), (512, 4), (2048, 16))
Q_BLOCK = 128
POOL_WINDOWS = (2, 4, 8, 16)
POOL_GROUP = D_MODEL // len(POOL_WINDOWS)
CONV_WIDTH = 3
LN_EPS = 1e-5
DEEPNORM_ALPHA = (2.0 * DEPTH) ** 0.25
DEEPNORM_BETA = (8.0 * DEPTH) ** -0.25

kernel_name = 'hybrid_interleaved_dilated_pool_fox_shortconv'


def _n_layers_of(kind):
    return len(range(kind, DEPTH, N_MIXERS))


def layer_norm(x, g, b):
    xf = x.astype(jnp.float32)
    mu = jnp.mean(xf, axis=-1, keepdims=True)
    xc = xf - mu
    var = jnp.mean(xc * xc, axis=-1, keepdims=True)
    y = xc * lax.rsqrt(var + LN_EPS) * g.astype(jnp.float32) + b.astype(jnp.float32)
    return y.astype(x.dtype)


def rope_tables(S):
    pos = jnp.arange(S, dtype=jnp.float32)
    inv = ROPE_THETA ** (-jnp.arange(0, HEAD_DIM, 2, dtype=jnp.float32) / HEAD_DIM)
    ang = pos[:, None] * inv[None, :]
    ang = jnp.concatenate([ang, ang], axis=-1)
    return jnp.cos(ang), jnp.sin(ang)


def apply_rope(t, cos, sin):
    tf = t.astype(jnp.float32)
    t1, t2 = jnp.split(tf, 2, axis=-1)
    rot = jnp.concatenate([-t2, t1], axis=-1)
    return (tf * cos[:, None, :] + rot * sin[:, None, :]).astype(t.dtype)


def dilated_branch(q, k, v, window, dilation):
    B, S, H, Dh = q.shape
    blk = window // dilation
    chunk = blk * dilation
    L = -(-S // chunk) * chunk
    nb = L // chunk

    def to_classes(t):
        t = jnp.pad(t, ((0, 0), (0, L - S), (0, 0), (0, 0)))
        t = t.reshape(B, nb * blk, dilation, H, Dh).transpose(0, 2, 1, 3, 4)
        return t.reshape(B, dilation, nb, blk, H, Dh)

    def with_prev(t):
        prev = jnp.pad(t[:, :, :-1], ((0, 0), (0, 0), (1, 0), (0, 0), (0, 0), (0, 0)))
        return jnp.concatenate([prev, t], axis=3)

    qs, ks, vs = to_classes(q), to_classes(k), to_classes(v)
    kk, vv = with_prev(ks), with_prev(vs)
    s = jnp.einsum('brnqhd,brnkhd->brnhqk', qs, kk).astype(jnp.float32)
    qi = jnp.arange(blk)[:, None]
    ki = jnp.arange(2 * blk)[None, :]
    rel = blk + qi - ki
    band = (rel >= 0) & (rel <= blk)
    has_prev = (jnp.arange(nb) > 0)[:, None, None] | (ki >= blk)[None]
    valid = band[None] & has_prev
    s = jnp.where(valid[None, None, :, None], s, -jnp.inf)
    m = jnp.max(s, axis=-1, keepdims=True)
    p = jnp.exp(s - m)
    den = jnp.sum(p, axis=-1)
    o = jnp.einsum('brnhqk,brnkhd->brnqhd', p, vv.astype(jnp.float32))
    o = o / jnp.swapaxes(den, -1, -2)[..., None]
    lse = jnp.swapaxes(m[..., 0] + jnp.log(den), -1, -2)

    def from_classes(t):
        t = t.reshape((B, dilation, nb * blk) + t.shape[4:])
        t = jnp.moveaxis(t, 1, 2).reshape((B, L) + t.shape[3:])
        return t[:, :S]

    return from_classes(o), from_classes(lse)


def mixer_dilated(x, wqkv, wo, cos, sin):
    B, S, D = x.shape
    q, k, v = jnp.split(x @ wqkv, 3, axis=-1)
    q = apply_rope(q.reshape(B, S, N_HEADS, HEAD_DIM), cos, sin) * (HEAD_DIM ** -0.5)
    k = apply_rope(k.reshape(B, S, N_HEADS, HEAD_DIM), cos, sin)
    v = v.reshape(B, S, N_HEADS, HEAD_DIM)
    outs, lses = [], []
    for window, dilation in DILATED_PATTERNS:
        o, l = dilated_branch(q, k, v, window, dilation)
        outs.append(o)
        lses.append(l)
    wts = jax.nn.softmax(jnp.stack(lses, axis=0), axis=0)
    o = jnp.sum(wts[..., None] * jnp.stack(outs, axis=0), axis=0)
    return o.reshape(B, S, D).astype(x.dtype) @ wo


def mixer_pool(x, wgrp, scale):
    B, S, D = x.shape
    groups = x.astype(jnp.float32).reshape(B, S, len(POOL_WINDOWS), POOL_GROUP)
    t = jnp.arange(1, S + 1, dtype=jnp.float32)
    pooled = []
    for g, w in enumerate(POOL_WINDOWS):
        xg = groups[:, :, g]
        cs = jnp.pad(jnp.cumsum(xg, axis=1), ((0, 0), (1, 0), (0, 0)))
        hi = cs[:, 1:]
        lo = jnp.pad(cs, ((0, 0), (w - 1, 0), (0, 0)))[:, :S]
        cnt = jnp.minimum(t, float(w))[None, :, None]
        pooled.append((hi - lo) / cnt - xg)
    pooled = jnp.stack(pooled, axis=2).astype(x.dtype)
    y = jnp.einsum('bsgc,gcd->bsgd', pooled, wgrp).reshape(B, S, D)
    return y * scale


def mixer_fox(x, win, bf, wo):
    B, S, D = x.shape
    proj = x @ win
    q = proj[..., :D].reshape(B, S, N_HEADS, HEAD_DIM) * (HEAD_DIM ** -0.5)
    k = proj[..., D:2 * D].reshape(B, S, N_HEADS, HEAD_DIM)
    v = proj[..., 2 * D:3 * D].reshape(B, S, N_HEADS, HEAD_DIM)
    logf = jax.nn.log_sigmoid(proj[..., 3 * D:].astype(jnp.float32) + bf.astype(jnp.float32))
    c = jnp.swapaxes(jnp.cumsum(logf, axis=1), 1, 2)
    nq = S // Q_BLOCK
    qb = q.reshape(B, nq, Q_BLOCK, N_HEADS, HEAD_DIM).transpose(1, 0, 2, 3, 4)
    cb = c.reshape(B, N_HEADS, nq, Q_BLOCK).transpose(2, 0, 1, 3)
    kpos = jnp.arange(S)

    def block(args):
        q_i, c_i, i = args
        s = jnp.einsum('bqhd,bkhd->bhqk', q_i, k).astype(jnp.float32)
        s = s + (c_i[..., :, None] - c[:, :, None, :])
        qpos = i * Q_BLOCK + jnp.arange(Q_BLOCK)
        s = jnp.where(kpos[None, :] <= qpos[:, None], s, -jnp.inf)
        p = jax.nn.softmax(s, axis=-1)
        return jnp.einsum('bhqk,bkhd->bqhd', p.astype(v.dtype), v)

    o = lax.map(block, (qb, cb, jnp.arange(nq)))
    o = o.transpose(1, 0, 2, 3, 4).reshape(B, S, D)
    return o @ wo


def mixer_conv(x, win, wconv, wout):
    D = x.shape[-1]
    b_gate, c_gate, h = jnp.split(x @ win, 3, axis=-1)
    z = c_gate * h
    conv = lax.conv_general_dilated(
        z, wconv[:, None, :].astype(z.dtype), window_strides=(1,),
        padding=[(CONV_WIDTH - 1, 0)], dimension_numbers=('NWC', 'WIO', 'NWC'),
        feature_group_count=D)
    return (b_gate * conv) @ wout


def mlp_sqrelu(x, w1, w2):
    h = jax.nn.relu(x @ w1)
    return (h * h) @ w2


def setup_inputs(seed: int = 0) -> dict:
    key = jax.random.key(seed)
    ks = jax.random.split(key, 20)
    D = D_MODEL
    f32 = jnp.float32

    def nrm(k, shape, scale):
        return jax.random.normal(k, shape, f32) * scale

    nA, nB, nC, nD = (_n_layers_of(i) for i in range(N_MIXERS))
    v_cols = jnp.concatenate([jnp.ones((2 * D,), f32), jnp.full((D,), DEEPNORM_BETA, f32)])
    fox_cols = jnp.concatenate([v_cols, jnp.ones((N_HEADS,), f32)])
    return {
        'x': nrm(ks[0], (BATCH, SEQ, D), 1.0),
        'ln_g': 1.0 + nrm(ks[1], (DEPTH, 2, D), 0.02),
        'ln_b': nrm(ks[2], (DEPTH, 2, D), 0.02),
        'mlp_w1': nrm(ks[3], (DEPTH, D, D_FF), D ** -0.5),
        'mlp_w2': nrm(ks[4], (DEPTH, D_FF, D), D_FF ** -0.5 * DEEPNORM_BETA),
        'a_wqkv': nrm(ks[5], (nA, D, 3 * D), D ** -0.5) * v_cols,
        'a_wo': nrm(ks[6], (nA, D, D), D ** -0.5 * DEEPNORM_BETA),
        'b_wgrp': nrm(ks[7], (nB, len(POOL_WINDOWS), POOL_GROUP, POOL_GROUP), POOL_GROUP ** -0.5 * DEEPNORM_BETA),
        'b_scale': 1.0 + nrm(ks[8], (nB, D), 0.02),
        'c_win': nrm(ks[9], (nC, D, 3 * D + N_HEADS), D ** -0.5) * fox_cols,
        'c_bf': jax.random.uniform(ks[10], (nC, N_HEADS), f32, 1.0, 5.0),
        'c_wo': nrm(ks[11], (nC, D, D), D ** -0.5 * DEEPNORM_BETA),
        'd_win': nrm(ks[12], (nD, D, 3 * D), D ** -0.5),
        'd_conv': nrm(ks[13], (nD, CONV_WIDTH, D), CONV_WIDTH ** -0.5),
        'd_wout': nrm(ks[14], (nD, D, D), D ** -0.5 * DEEPNORM_BETA),
    }


def reference(x, ln_g, ln_b, mlp_w1, mlp_w2, a_wqkv, a_wo, b_wgrp, b_scale,
              c_win, c_bf, c_wo, d_win, d_conv, d_wout):
    S = x.shape[1]
    cos, sin = rope_tables(S)
    for i in range(DEPTH):
        kind, j = i % N_MIXERS, i // N_MIXERS
        if kind == 0:
            y = mixer_dilated(x, a_wqkv[j], a_wo[j], cos, sin)
        elif kind == 1:
            y = mixer_pool(x, b_wgrp[j], b_scale[j])
        elif kind == 2:
            y = mixer_fox(x, c_win[j], c_bf[j], c_wo[j])
        else:
            y = mixer_conv(x, d_win[j], d_conv[j], d_wout[j])
        x = layer_norm(DEEPNORM_ALPHA * x + y, ln_g[i, 0], ln_b[i, 0])
        x = layer_norm(DEEPNORM_ALPHA * x + mlp_sqrelu(x, mlp_w1[i], mlp_w2[i]), ln_g[i, 1], ln_b[i, 1])
    return x
```

```python
import functools

import jax
import jax.numpy as jnp
from jax import lax
from jax.experimental import pallas as pl
from jax.experimental.pallas import tpu as pltpu

N_HEADS = 32
ROPE_THETA = 10000.0
DILATED_PATTERNS = ((128, 1), (512, 4), (2048, 16))
POOL_WINDOWS = (2, 4, 8, 16)
CONV_WIDTH = 3
LN_EPS = 1e-5
LANES = 128
HALO_ROWS = 16
VMEM_LIMIT_BYTES = 56 * 1024 * 1024
MASK_VALUE = -1e30

BF16 = jnp.bfloat16
F32 = jnp.float32


def _tile(n, pref):
    t = min(n, pref)
    while n % t:
        assert t % 2 == 0, (n, pref)
        t //= 2
    return t


def _params(*sem):
    return pltpu.CompilerParams(dimension_semantics=sem, vmem_limit_bytes=VMEM_LIMIT_BYTES)


def _mm_body(*refs, act, has_scale, nk):
    x_ref, w_ref = refs[0], refs[1]
    o_ref = refs[-1]
    acc = jnp.dot(x_ref[...], w_ref[...], preferred_element_type=F32)
    if nk == 1:
        if act == "relu2":
            acc = jnp.maximum(acc, 0.0)
            acc = acc * acc
        if has_scale:
            acc = acc * refs[2][...]
        o_ref[...] = acc.astype(o_ref.dtype)
    else:
        k = pl.program_id(2)

        @pl.when(k == 0)
        def _():
            o_ref[...] = acc

        @pl.when(k > 0)
        def _():
            o_ref[...] += acc


def _matmul(x, w, *, out_dtype, act=None, col_scale=None, bm=1024, bn=1024, bk=None):
    M, K = x.shape
    N = w.shape[1]
    bm, bn = _tile(M, bm), _tile(N, bn)
    bk = K if bk is None else _tile(K, bk)
    nk = K // bk
    if nk > 1:
        assert act is None and col_scale is None and out_dtype == F32
    in_specs = [
        pl.BlockSpec((bm, bk), lambda i, j, k: (i, k)),
        pl.BlockSpec((bk, bn), lambda i, j, k: (k, j)),
    ]
    args = [x, w]
    if col_scale is not None:
        in_specs.append(pl.BlockSpec((1, bn), lambda i, j, k: (0, j)))
        args.append(col_scale.reshape(1, N).astype(F32))
    return pl.pallas_call(
        functools.partial(_mm_body, act=act, has_scale=col_scale is not None, nk=nk),
        grid=(M // bm, N // bn, nk),
        in_specs=in_specs,
        out_specs=pl.BlockSpec((bm, bn), lambda i, j, k: (i, j)),
        out_shape=jax.ShapeDtypeStruct((M, N), out_dtype),
        compiler_params=_params("parallel", "parallel", "arbitrary"),
        name="matmul",
    )(*args)


def _mm_rope_body(x_ref, w_ref, cos_ref, sin_ref, o_ref):
    acc = jnp.dot(x_ref[...], w_ref[...], preferred_element_type=F32)
    cos, sin = cos_ref[0], sin_ref[0]
    for h in range(acc.shape[1] // LANES):
        cs = slice(h * LANES, (h + 1) * LANES)
        a = acc[:, cs]
        o_ref[:, cs] = (a * cos + pltpu.roll(a, LANES // 2, 1) * sin).astype(o_ref.dtype)


def _matmul_rope(x, w, cos_tab, sin_tab, seq, *, bm=1024, bn=1024):
    M, K = x.shape
    N = w.shape[1]
    bm, bn = _tile(seq, bm), _tile(N // 2, bn)
    tiles_per_part = (N // 2) // bn
    seq_tiles = seq // bm
    tab_spec = pl.BlockSpec((1, bm, LANES), lambda i, j: (j // tiles_per_part, i % seq_tiles, 0))
    return pl.pallas_call(
        _mm_rope_body,
        grid=(M // bm, N // bn),
        in_specs=[
            pl.BlockSpec((bm, K), lambda i, j: (i, 0)),
            pl.BlockSpec((K, bn), lambda i, j: (0, j)),
            tab_spec,
            tab_spec,
        ],
        out_specs=pl.BlockSpec((bm, bn), lambda i, j: (i, j)),
        out_shape=jax.ShapeDtypeStruct((M, N), BF16),
        compiler_params=_params("parallel", "parallel"),
        name="matmul_rope",
    )(x, w, cos_tab, sin_tab)


def _layer_norm_rows(v, g, b):
    mu = jnp.mean(v, axis=-1, keepdims=True)
    xc = v - mu
    var = jnp.mean(xc * xc, axis=-1, keepdims=True)
    return xc * lax.rsqrt(var + LN_EPS) * g + b


def _ln_res_body(x_ref, y_ref, g_ref, b_ref, of_ref, ob_ref, *, alpha):
    out = _layer_norm_rows(alpha * x_ref[...] + y_ref[...], g_ref[...], b_ref[...])
    of_ref[...] = out
    ob_ref[...] = out.astype(BF16)


def _ln_residual(x, y, g, b, alpha, *, tm=256):
    M, D = x.shape
    tm = _tile(M, tm)
    row = pl.BlockSpec((tm, D), lambda i: (i, 0))
    vec = pl.BlockSpec((1, D), lambda i: (0, 0))
    return pl.pallas_call(
        functools.partial(_ln_res_body, alpha=alpha),
        grid=(M // tm,),
        in_specs=[row, row, vec, vec],
        out_specs=[row, row],
        out_shape=[jax.ShapeDtypeStruct((M, D), F32), jax.ShapeDtypeStruct((M, D), BF16)],
        compiler_params=_params("parallel"),
        name="ln_residual",
    )(x, y, g.reshape(1, D), b.reshape(1, D))


def _dilated_body(q_ref, k_ref, v_ref, o_ref, lse_ref, *, heads, blk):
    hg = pl.program_id(2)
    n_blocks = q_ref.shape[1] // blk

    @pl.when(hg == 0)
    def _():
        lse_ref[...] = jnp.zeros_like(lse_ref)

    lane = lax.broadcasted_iota(jnp.int32, (blk, LANES), 1)
    qi = lax.broadcasted_iota(jnp.int32, (blk, 2 * blk), 0)
    kj = lax.broadcasted_iota(jnp.int32, (blk, 2 * blk), 1)
    band = (kj >= qi) & (kj <= qi + blk)
    qi1 = lax.broadcasted_iota(jnp.int32, (blk, blk), 0)
    kj1 = lax.broadcasted_iota(jnp.int32, (blk, blk), 1)
    tri = kj1 <= qi1
    nt = (((1,), (1,)), ((), ()))

    for hh in range(heads):
        cs = slice(hh * LANES, (hh + 1) * LANES)
        head_lane = hg * heads + hh

        def finish(r0, s, v, cs=cs, head_lane=head_lane):
            m = jnp.max(s, axis=1, keepdims=True)
            p = jnp.exp(s - m)
            den = jnp.sum(p, axis=1, keepdims=True)
            o = jnp.dot(p.astype(BF16), v, preferred_element_type=F32)
            o_ref[0, pl.ds(r0, blk), cs] = o / den
            lse = m + jnp.log(den)
            rows = pl.ds(r0, blk)
            lse_ref[0, rows, :] = jnp.where(lane == head_lane, lse, lse_ref[0, rows, :])

        s0 = lax.dot_general(q_ref[0, 0:blk, cs], k_ref[0, 0:blk, cs], nt, preferred_element_type=F32)
        finish(0, jnp.where(tri, s0, MASK_VALUE), v_ref[0, 0:blk, cs])

        def body(n, carry, cs=cs, finish=finish):
            r0 = pl.multiple_of(n * blk, blk)
            keys = pl.ds(r0 - blk, 2 * blk)
            s = lax.dot_general(q_ref[0, pl.ds(r0, blk), cs], k_ref[0, keys, cs], nt, preferred_element_type=F32)
            finish(r0, jnp.where(band, s, MASK_VALUE), v_ref[0, keys, cs])
            return carry

        lax.fori_loop(1, n_blocks, body, 0)


def _dilated_branch(qk, v, B, S, D, window, dilation, heads):
    blk = window // dilation
    assert S % (blk * dilation) == 0
    sc = S // dilation
    W = heads * LANES
    qk_v = qk.reshape(B, sc, dilation * 2 * D)
    v_v = v.reshape(B, sc, dilation * D)
    qk_tiles, v_tiles = 2 * D // W, D // W
    o, lse = pl.pallas_call(
        functools.partial(_dilated_body, heads=heads, blk=blk),
        grid=(B, dilation, D // W),
        in_specs=[
            pl.BlockSpec((1, sc, W), lambda b, r, g: (b, 0, r * qk_tiles + g)),
            pl.BlockSpec((1, sc, W), lambda b, r, g: (b, 0, r * qk_tiles + v_tiles + g)),
            pl.BlockSpec((1, sc, W), lambda b, r, g: (b, 0, r * v_tiles + g)),
        ],
        out_specs=[
            pl.BlockSpec((1, sc, W), lambda b, r, g: (b, 0, r * v_tiles + g)),
            pl.BlockSpec((1, sc, LANES), lambda b, r, g: (b, 0, r)),
        ],
        out_shape=[
            jax.ShapeDtypeStruct((B, sc, dilation * D), F32),
            jax.ShapeDtypeStruct((B, sc, dilation * LANES), F32),
        ],
        compiler_params=_params("parallel", "parallel", "arbitrary"),
        name=f"dilated_d{dilation}",
    )(qk_v, qk_v, v_v)
    return o.reshape(B, S, D), lse.reshape(B, S, LANES)


def _merge_body(*refs, n_heads, n_branch):
    o_refs, l_refs, out_ref = refs[:n_branch], refs[n_branch : 2 * n_branch], refs[-1]
    lses = [r[0] for r in l_refs]
    m = functools.reduce(jnp.maximum, lses)
    es = [jnp.exp(l - m) for l in lses]
    tot = functools.reduce(lambda a, b: a + b, es)
    ws = [e / tot for e in es]
    ts = out_ref.shape[1]
    for h in range(n_heads):
        cs = slice(h * LANES, (h + 1) * LANES)
        acc = None
        for w, o_ref in zip(ws, o_refs):
            term = jnp.broadcast_to(w[:, h : h + 1], (ts, LANES)) * o_ref[0, :, cs]
            acc = term if acc is None else acc + term
        out_ref[0, :, cs] = acc.astype(out_ref.dtype)


def _merge_branches(outs, lses, n_heads, *, ts=256):
    B, S, D = outs[0].shape
    ts = _tile(S, ts)
    n = len(outs)
    o_spec = pl.BlockSpec((1, ts, D), lambda b, i: (b, i, 0))
    l_spec = pl.BlockSpec((1, ts, LANES), lambda b, i: (b, i, 0))
    return pl.pallas_call(
        functools.partial(_merge_body, n_heads=n_heads, n_branch=n),
        grid=(B, S // ts),
        in_specs=[o_spec] * n + [l_spec] * n,
        out_specs=o_spec,
        out_shape=jax.ShapeDtypeStruct((B, S, D), BF16),
        compiler_params=_params("parallel", "parallel"),
        name="dilated_merge",
    )(*outs, *lses)


def _rope_tables(S, hd, q_scale):
    pos = jnp.arange(S, dtype=F32)
    inv = ROPE_THETA ** (-jnp.arange(0, hd, 2, dtype=F32) / hd)
    ang = pos[:, None] * inv[None, :]
    ang = jnp.concatenate([ang, ang], axis=-1)
    sign = jnp.concatenate([-jnp.ones((hd // 2,), F32), jnp.ones((hd // 2,), F32)])
    cos, sin = jnp.cos(ang), jnp.sin(ang) * sign
    return jnp.stack([cos * q_scale, cos]), jnp.stack([sin * q_scale, sin])


def _mixer_dilated(xb, wqkv, wo, B, S, D):
    hd = D // N_HEADS
    assert hd == LANES
    cos_tab, sin_tab = _rope_tables(S, hd, hd**-0.5)
    wb = wqkv.astype(BF16)
    qk = _matmul_rope(xb, wb[:, : 2 * D], cos_tab, sin_tab, S)
    v = _matmul(xb, wb[:, 2 * D :], out_dtype=BF16)
    outs, lses = [], []
    for (window, dilation), heads in zip(DILATED_PATTERNS, (2, 4, 8)):
        o, l = _dilated_branch(qk, v, B, S, D, window, dilation, min(heads, N_HEADS))
        outs.append(o)
        lses.append(l)
    o = _merge_branches(outs, lses, N_HEADS)
    return _matmul(o.reshape(B * S, D), wo.astype(BF16), out_dtype=F32)


def _pool_ln_body(x_ref, xh_ref, w_ref, sc_ref, g_ref, b_ref, of_ref, ob_ref, xs_ref, y_ref, *, alpha, ts):
    i = pl.program_id(1)
    x = x_ref[0]
    xs_ref[0:HALO_ROWS, :] = jnp.where(i > 0, xh_ref[0], 0.0)
    xs_ref[HALO_ROWS:, :] = x
    cg = x.shape[1] // len(POOL_WINDOWS)
    t1 = (i * ts + 1 + lax.broadcasted_iota(jnp.int32, (ts, 1), 0)).astype(F32)
    for g, win in enumerate(POOL_WINDOWS):
        cs = slice(g * cg, (g + 1) * cg)
        tot = x[:, cs]
        for j in range(1, win):
            tot = tot + xs_ref[pl.ds(HALO_ROWS - j, ts), cs]
        pooled = tot / jnp.minimum(t1, float(win)) - x[:, cs]
        y = jnp.dot(pooled.astype(BF16), w_ref[g], preferred_element_type=F32)
        y_ref[:, cs] = y * sc_ref[:, cs]
    out = _layer_norm_rows(alpha * x + y_ref[...], g_ref[...], b_ref[...])
    of_ref[0] = out
    ob_ref[0] = out.astype(BF16)


def _mixer_pool_ln(xf, wgrp, scale, g, b, alpha, B, S, D, *, ts=256):
    ts = _tile(S, ts)
    assert ts % HALO_ROWS == 0 and max(POOL_WINDOWS) <= HALO_ROWS
    G, cg, _ = wgrp.shape
    x3 = xf.reshape(B, S, D)
    row = pl.BlockSpec((1, ts, D), lambda bb, i: (bb, i, 0))
    halo = pl.BlockSpec((1, HALO_ROWS, D), lambda bb, i: (bb, jnp.maximum(i * (ts // HALO_ROWS) - 1, 0), 0))
    vec = pl.BlockSpec((1, D), lambda bb, i: (0, 0))
    of, ob = pl.pallas_call(
        functools.partial(_pool_ln_body, alpha=alpha, ts=ts),
        grid=(B, S // ts),
        in_specs=[row, halo, pl.BlockSpec((G, cg, cg), lambda bb, i: (0, 0, 0)), vec, vec, vec],
        out_specs=[row, row],
        out_shape=[jax.ShapeDtypeStruct((B, S, D), F32), jax.ShapeDtypeStruct((B, S, D), BF16)],
        scratch_shapes=[pltpu.VMEM((ts + HALO_ROWS, D), F32), pltpu.VMEM((ts, D), F32)],
        compiler_params=_params("parallel", "parallel"),
        name="pool_ln",
    )(x3, x3, wgrp.astype(BF16), scale.reshape(1, D), g.reshape(1, D), b.reshape(1, D))
    return of.reshape(B * S, D), ob.reshape(B * S, D)


def _fgate_body(z_ref, bf_ref, c_ref):
    z = z_ref[0] + bf_ref[...]
    c = jnp.minimum(z, 0.0) - jnp.log1p(jnp.exp(-jnp.abs(z)))
    S = c.shape[0]
    row = lax.broadcasted_iota(jnp.int32, c.shape, 0)
    shift = 1
    while shift < S:
        c = c + jnp.where(row >= shift, pltpu.roll(c, shift, 0), 0.0)
        shift *= 2
    c_ref[0] = c


def _forget_cumsum(z, bias_row):
    B, S, W = z.shape
    blk = pl.BlockSpec((1, S, W), lambda b: (b, 0, 0))
    return pl.pallas_call(
        _fgate_body,
        grid=(B,),
        in_specs=[blk, pl.BlockSpec((1, W), lambda b: (0, 0))],
        out_specs=blk,
        out_shape=jax.ShapeDtypeStruct((B, S, W), F32),
        compiler_params=_params("parallel"),
        name="forget_cumsum",
    )(z, bias_row)


def _fox_body(q_ref, k_ref, v_ref, cc_ref, cr_ref, o_ref, *, bq):
    h = pl.program_id(1)
    S = q_ref.shape[1]
    lane = lax.broadcasted_iota(jnp.int32, (bq, LANES), 1)
    qi = lax.broadcasted_iota(jnp.int32, (bq, bq), 0)
    kj = lax.broadcasted_iota(jnp.int32, (bq, bq), 1)
    tri = kj <= qi
    nt = (((1,), (1,)), ((), ()))

    def q_block(i, carry):
        r0 = pl.multiple_of(i * bq, bq)
        q = q_ref[0, pl.ds(r0, bq), :]
        cq = jnp.sum(jnp.where(lane == h, cc_ref[0, pl.ds(r0, bq), :], 0.0), axis=1, keepdims=True)

        def k_step(j, state, masked):
            m, l, acc = state
            c0 = pl.multiple_of(j * bq, bq)
            s = lax.dot_general(q, k_ref[0, pl.ds(c0, bq), :], nt, preferred_element_type=F32)
            s = s + (cq - cr_ref[0, :, pl.ds(c0, bq)])
            if masked:
                s = jnp.where(tri, s, MASK_VALUE)
            m_new = jnp.maximum(m, jnp.max(s, axis=1, keepdims=True))
            a = jnp.exp(m - m_new)
            p = jnp.exp(s - m_new)
            l = a * l + jnp.sum(p, axis=1, keepdims=True)
            acc = a * acc + jnp.dot(p.astype(BF16), v_ref[0, pl.ds(c0, bq), :], preferred_element_type=F32)
            return m_new, l, acc

        init = (jnp.full((bq, 1), MASK_VALUE, F32), jnp.zeros((bq, 1), F32), jnp.zeros((bq, LANES), F32))
        state = lax.fori_loop(0, i, lambda j, st: k_step(j, st, False), init)
        _, l, acc = k_step(i, state, True)
        o_ref[0, pl.ds(r0, bq), :] = (acc / l).astype(o_ref.dtype)
        return carry

    lax.fori_loop(0, S // bq, q_block, 0)


def _fox_attention(proj, c_col, c_row, B, S, D, *, bq=256):
    bq = _tile(S, bq)
    H = D // LANES
    head = lambda off: pl.BlockSpec((1, S, LANES), lambda b, h: (b, 0, off + h))
    return pl.pallas_call(
        functools.partial(_fox_body, bq=bq),
        grid=(B, H),
        in_specs=[
            head(0),
            head(H),
            head(2 * H),
            pl.BlockSpec((1, S, LANES), lambda b, h: (b, 0, 0)),
            pl.BlockSpec((1, 1, S), lambda b, h: (b * H + h, 0, 0)),
        ],
        out_specs=pl.BlockSpec((1, S, LANES), lambda b, h: (b, 0, h)),
        out_shape=jax.ShapeDtypeStruct((B, S, D), BF16),
        compiler_params=_params("parallel", "arbitrary"),
        name="fox_attention",
    )(proj, proj, proj, c_col, c_row)


def _mixer_fox(xb, win, bf, wo, B, S, D):
    H = N_HEADS
    assert D // H == LANES and H <= LANES
    wb = win.astype(BF16)
    col_scale = jnp.concatenate([jnp.full((D,), LANES**-0.5, F32), jnp.ones((2 * D,), F32)])
    proj = _matmul(xb, wb[:, : 3 * D], out_dtype=BF16, col_scale=col_scale)
    w_gate = jnp.pad(wb[:, 3 * D :], ((0, 0), (0, LANES - H)))
    z = _matmul(xb, w_gate, out_dtype=F32, bn=LANES)
    bias_row = jnp.pad(bf.astype(F32), (0, LANES - H)).reshape(1, LANES)
    c_col = _forget_cumsum(z.reshape(B, S, LANES), bias_row)
    c_row = jnp.swapaxes(c_col[:, :, :H], 1, 2).reshape(B * H, 1, S)
    o = _fox_attention(proj.reshape(B, S, 3 * D), c_col, c_row, B, S, D)
    return _matmul(o.reshape(B * S, D), wo.astype(BF16), out_dtype=F32)


def _convgate_body(b_ref, c_ref, h_ref, ch_ref, hh_ref, w_ref, o_ref, zs_ref, *, ts):
    i = pl.program_id(1)
    z = c_ref[0].astype(F32) * h_ref[0].astype(F32)
    zh = ch_ref[0].astype(F32) * hh_ref[0].astype(F32)
    zs_ref[0:HALO_ROWS, :] = jnp.where(i > 0, zh, 0.0)
    zs_ref[HALO_ROWS:, :] = z
    w = w_ref[...]
    conv = w[CONV_WIDTH - 1 : CONV_WIDTH] * z
    for j in range(1, CONV_WIDTH):
        tap = CONV_WIDTH - 1 - j
        conv = conv + w[tap : tap + 1] * zs_ref[pl.ds(HALO_ROWS - j, ts), :]
    o_ref[0] = (b_ref[0].astype(F32) * conv).astype(o_ref.dtype)


def _conv_gate(proj, wconv, B, S, D, *, ts=512, tc=1024):
    ts, tc = _tile(S, ts), _tile(D, tc)
    assert ts % HALO_ROWS == 0 and CONV_WIDTH - 1 <= HALO_ROWS
    nct = D // tc
    cur = lambda part: pl.BlockSpec((1, ts, tc), lambda b, i, j: (b, i, part * nct + j))
    halo = lambda part: pl.BlockSpec(
        (1, HALO_ROWS, tc), lambda b, i, j: (b, jnp.maximum(i * (ts // HALO_ROWS) - 1, 0), part * nct + j)
    )
    return pl.pallas_call(
        functools.partial(_convgate_body, ts=ts),
        grid=(B, S // ts, nct),
        in_specs=[cur(0), cur(1), cur(2), halo(1), halo(2), pl.BlockSpec((CONV_WIDTH, tc), lambda b, i, j: (0, j))],
        out_specs=pl.BlockSpec((1, ts, tc), lambda b, i, j: (b, i, j)),
        out_shape=jax.ShapeDtypeStruct((B, S, D), BF16),
        scratch_shapes=[pltpu.VMEM((ts + HALO_ROWS, tc), F32)],
        compiler_params=_params("parallel", "parallel", "parallel"),
        name="conv_gate",
    )(proj, proj, proj, proj, proj, wconv.astype(F32))


def _mixer_conv(xb, win, wconv, wout, B, S, D):
    proj = _matmul(xb, win.astype(BF16), out_dtype=BF16)
    u = _conv_gate(proj.reshape(B, S, 3 * D), wconv, B, S, D)
    return _matmul(u.reshape(B * S, D), wout.astype(BF16), out_dtype=F32)


def kernel(x, ln_g, ln_b, mlp_w1, mlp_w2, a_wqkv, a_wo, b_wgrp, b_scale, c_win, c_bf, c_wo, d_win, d_conv, d_wout):
    B, S, D = x.shape
    depth = ln_g.shape[0]
    n_mixers = 4
    alpha = (2.0 * depth) ** 0.25
    xf = x.reshape(B * S, D).astype(F32)
    xb = xf.astype(BF16)
    for i in range(depth):
        kind, j = i % n_mixers, i // n_mixers
        if kind == 1:
            xf, xb = _mixer_pool_ln(xf, b_wgrp[j], b_scale[j], ln_g[i, 0], ln_b[i, 0], alpha, B, S, D)
        else:
            if kind == 0:
                y = _mixer_dilated(xb, a_wqkv[j], a_wo[j], B, S, D)
            elif kind == 2:
                y = _mixer_fox(xb, c_win[j], c_bf[j], c_wo[j], B, S, D)
            else:
                y = _mixer_conv(xb, d_win[j], d_conv[j], d_wout[j], B, S, D)
            xf, xb = _ln_residual(xf, y, ln_g[i, 0], ln_b[i, 0], alpha)
        hid = _matmul(xb, mlp_w1[i].astype(BF16), out_dtype=BF16, act="relu2")
        y = _matmul(hid, mlp_w2[i].astype(BF16), out_dtype=F32, bk=2048)
        xf, xb = _ln_residual(xf, y, ln_g[i, 1], ln_b[i, 1], alpha)
    return xf.reshape(B, S, D).astype(x.dtype)
```

```python
import functools

import jax
import jax.numpy as jnp
from jax import lax
from jax.experimental import pallas as pl
from jax.experimental.pallas import tpu as pltpu

N_HEADS = 32
ROPE_THETA = 10000.0
DILATED_PATTERNS = ((128, 1), (512, 4), (2048, 16))
POOL_WINDOWS = (2, 4, 8, 16)
CONV_WIDTH = 3
LN_EPS = 1e-5
LANES = 128
HALO_ROWS = 16
VMEM_LIMIT_BYTES = 56 * 1024 * 1024
MASK_VALUE = -1e30

BF16 = jnp.bfloat16
F32 = jnp.float32


def _tile(n, pref):
    t = min(n, pref)
    while n % t:
        assert t % 2 == 0, (n, pref)
        t //= 2
    return t


def _params(*sem):
    return pltpu.CompilerParams(dimension_semantics=sem, vmem_limit_bytes=VMEM_LIMIT_BYTES)


def _mm_body(*refs, act, has_scale, nk):
    x_ref, w_ref = refs[0], refs[1]
    o_ref = refs[-1]
    acc = jnp.dot(x_ref[...], w_ref[...], preferred_element_type=F32)
    if nk == 1:
        if act == "relu2":
            acc = jnp.maximum(acc, 0.0)
            acc = acc * acc
        if has_scale:
            acc = acc * refs[2][...]
        o_ref[...] = acc.astype(o_ref.dtype)
    else:
        k = pl.program_id(2)

        @pl.when(k == 0)
        def _():
            o_ref[...] = acc

        @pl.when(k > 0)
        def _():
            o_ref[...] += acc


def _matmul(x, w, *, out_dtype, act=None, col_scale=None, bm=1024, bn=1024, bk=None):
    M, K = x.shape
    N = w.shape[1]
    bm, bn = _tile(M, bm), _tile(N, bn)
    bk = K if bk is None else _tile(K, bk)
    nk = K // bk
    if nk > 1:
        assert act is None and col_scale is None and out_dtype == F32
    in_specs = [
        pl.BlockSpec((bm, bk), lambda i, j, k: (i, k)),
        pl.BlockSpec((bk, bn), lambda i, j, k: (k, j)),
    ]
    args = [x, w]
    if col_scale is not None:
        in_specs.append(pl.BlockSpec((1, bn), lambda i, j, k: (0, j)))
        args.append(col_scale.reshape(1, N).astype(F32))
    return pl.pallas_call(
        functools.partial(_mm_body, act=act, has_scale=col_scale is not None, nk=nk),
        grid=(M // bm, N // bn, nk),
        in_specs=in_specs,
        out_specs=pl.BlockSpec((bm, bn), lambda i, j, k: (i, j)),
        out_shape=jax.ShapeDtypeStruct((M, N), out_dtype),
        compiler_params=_params("parallel", "parallel", "arbitrary"),
        name="matmul",
    )(*args)


def _mm_rope_body(x_ref, w_ref, cos_ref, sin_ref, o_ref):
    acc = jnp.dot(x_ref[...], w_ref[...], preferred_element_type=F32)
    cos, sin = cos_ref[0], sin_ref[0]
    for h in range(acc.shape[1] // LANES):
        cs = slice(h * LANES, (h + 1) * LANES)
        a = acc[:, cs]
        o_ref[:, cs] = (a * cos + pltpu.roll(a, LANES // 2, 1) * sin).astype(o_ref.dtype)


def _matmul_rope(x, w, cos_tab, sin_tab, seq, *, bm=1024, bn=1024):
    M, K = x.shape
    N = w.shape[1]
    bm, bn = _tile(seq, bm), _tile(N // 2, bn)
    tiles_per_part = (N // 2) // bn
    seq_tiles = seq // bm
    tab_spec = pl.BlockSpec((1, bm, LANES), lambda i, j: (j // tiles_per_part, i % seq_tiles, 0))
    return pl.pallas_call(
        _mm_rope_body,
        grid=(M // bm, N // bn),
        in_specs=[
            pl.BlockSpec((bm, K), lambda i, j: (i, 0)),
            pl.BlockSpec((K, bn), lambda i, j: (0, j)),
            tab_spec,
            tab_spec,
        ],
        out_specs=pl.BlockSpec((bm, bn), lambda i, j: (i, j)),
        out_shape=jax.ShapeDtypeStruct((M, N), BF16),
        compiler_params=_params("parallel", "parallel"),
        name="matmul_rope",
    )(x, w, cos_tab, sin_tab)


def _layer_norm_rows(v, g, b):
    mu = jnp.mean(v, axis=-1, keepdims=True)
    xc = v - mu
    var = jnp.mean(xc * xc, axis=-1, keepdims=True)
    return xc * lax.rsqrt(var + LN_EPS) * g + b


def _ln_res_body(x_ref, y_ref, g_ref, b_ref, of_ref, ob_ref, *, alpha):
    out = _layer_norm_rows(alpha * x_ref[...] + y_ref[...], g_ref[...], b_ref[...])
    of_ref[...] = out
    ob_ref[...] = out.astype(BF16)


def _ln_residual(x, y, g, b, alpha, *, tm=256):
    M, D = x.shape
    tm = _tile(M, tm)
    row = pl.BlockSpec((tm, D), lambda i: (i, 0))
    vec = pl.BlockSpec((1, D), lambda i: (0, 0))
    return pl.pallas_call(
        functools.partial(_ln_res_body, alpha=alpha),
        grid=(M // tm,),
        in_specs=[row, row, vec, vec],
        out_specs=[row, row],
        out_shape=[jax.ShapeDtypeStruct((M, D), F32), jax.ShapeDtypeStruct((M, D), BF16)],
        compiler_params=_params("parallel"),
        name="ln_residual",
    )(x, y, g.reshape(1, D), b.reshape(1, D))


def _dilated_body(q_ref, k_ref, v_ref, o_ref, lse_ref, *, heads, blk):
    hg = pl.program_id(2)
    n_blocks = q_ref.shape[1] // blk

    @pl.when(hg == 0)
    def _():
        lse_ref[...] = jnp.zeros_like(lse_ref)

    lane = lax.broadcasted_iota(jnp.int32, (blk, LANES), 1)
    qi = lax.broadcasted_iota(jnp.int32, (blk, 2 * blk), 0)
    kj = lax.broadcasted_iota(jnp.int32, (blk, 2 * blk), 1)
    band = (kj >= qi) & (kj <= qi + blk)
    qi1 = lax.broadcasted_iota(jnp.int32, (blk, blk), 0)
    kj1 = lax.broadcasted_iota(jnp.int32, (blk, blk), 1)
    tri = kj1 <= qi1
    nt = (((1,), (1,)), ((), ()))

    def block(rows, keys, mask):
        cols = [slice(hh * LANES, (hh + 1) * LANES) for hh in range(heads)]
        scores = [
            lax.dot_general(q_ref[0, rows, cs], k_ref[0, keys, cs], nt, preferred_element_type=F32) for cs in cols
        ]
        lse_tile = lse_ref[0, rows, :]
        probs, dens = [], []
        for hh, s in enumerate(scores):
            s = jnp.where(mask, s, MASK_VALUE)
            m = jnp.max(s, axis=1, keepdims=True)
            p = jnp.exp(s - m)
            den = jnp.sum(p, axis=1, keepdims=True)
            lse_tile = jnp.where(lane == hg * heads + hh, m + jnp.log(den), lse_tile)
            probs.append(p.astype(BF16))
            dens.append(den)
        lse_ref[0, rows, :] = lse_tile
        for cs, p, den in zip(cols, probs, dens):
            o = jnp.dot(p, v_ref[0, keys, cs], preferred_element_type=F32)
            o_ref[0, rows, cs] = (o / den).astype(o_ref.dtype)

    def later_block(n, carry):
        r0 = pl.multiple_of(n * blk, blk)
        block(pl.ds(r0, blk), pl.ds(r0 - blk, 2 * blk), band)
        return carry

    block(pl.ds(0, blk), pl.ds(0, blk), tri)
    lax.fori_loop(1, n_blocks, later_block, 0)


def _dilated_branch(qk, v, B, S, D, window, dilation, heads):
    blk = window // dilation
    assert S % (blk * dilation) == 0
    sc = S // dilation
    W = heads * LANES
    qk_v = qk.reshape(B, sc, dilation * 2 * D)
    v_v = v.reshape(B, sc, dilation * D)
    qk_tiles, v_tiles = 2 * D // W, D // W
    o, lse = pl.pallas_call(
        functools.partial(_dilated_body, heads=heads, blk=blk),
        grid=(B, dilation, D // W),
        in_specs=[
            pl.BlockSpec((1, sc, W), lambda b, r, g: (b, 0, r * qk_tiles + g)),
            pl.BlockSpec((1, sc, W), lambda b, r, g: (b, 0, r * qk_tiles + v_tiles + g)),
            pl.BlockSpec((1, sc, W), lambda b, r, g: (b, 0, r * v_tiles + g)),
        ],
        out_specs=[
            pl.BlockSpec((1, sc, W), lambda b, r, g: (b, 0, r * v_tiles + g)),
            pl.BlockSpec((1, sc, LANES), lambda b, r, g: (b, 0, r)),
        ],
        out_shape=[
            jax.ShapeDtypeStruct((B, sc, dilation * D), BF16),
            jax.ShapeDtypeStruct((B, sc, dilation * LANES), F32),
        ],
        compiler_params=_params("parallel", "parallel", "arbitrary"),
        name=f"dilated_d{dilation}",
    )(qk_v, qk_v, v_v)
    return o.reshape(B, S, D), lse.reshape(B, S, LANES)


def _merge_body(*refs, n_heads, n_branch):
    o_refs, l_refs, out_ref = refs[:n_branch], refs[n_branch : 2 * n_branch], refs[-1]
    lses = [r[0] for r in l_refs]
    m = functools.reduce(jnp.maximum, lses)
    es = [jnp.exp(l - m) for l in lses]
    tot = functools.reduce(lambda a, b: a + b, es)
    ws = [e / tot for e in es]
    ts = out_ref.shape[1]
    for h in range(n_heads):
        cs = slice(h * LANES, (h + 1) * LANES)
        acc = None
        for w, o_ref in zip(ws, o_refs):
            term = jnp.broadcast_to(w[:, h : h + 1], (ts, LANES)) * o_ref[0, :, cs].astype(F32)
            acc = term if acc is None else acc + term
        out_ref[0, :, cs] = acc.astype(out_ref.dtype)


def _merge_branches(outs, lses, n_heads, *, ts=256):
    B, S, D = outs[0].shape
    ts = _tile(S, ts)
    n = len(outs)
    o_spec = pl.BlockSpec((1, ts, D), lambda b, i: (b, i, 0))
    l_spec = pl.BlockSpec((1, ts, LANES), lambda b, i: (b, i, 0))
    return pl.pallas_call(
        functools.partial(_merge_body, n_heads=n_heads, n_branch=n),
        grid=(B, S // ts),
        in_specs=[o_spec] * n + [l_spec] * n,
        out_specs=o_spec,
        out_shape=jax.ShapeDtypeStruct((B, S, D), BF16),
        compiler_params=_params("parallel", "parallel"),
        name="dilated_merge",
    )(*outs, *lses)


def _rope_tables(S, hd, q_scale):
    pos = jnp.arange(S, dtype=F32)
    inv = ROPE_THETA ** (-jnp.arange(0, hd, 2, dtype=F32) / hd)
    ang = pos[:, None] * inv[None, :]
    ang = jnp.concatenate([ang, ang], axis=-1)
    sign = jnp.concatenate([-jnp.ones((hd // 2,), F32), jnp.ones((hd // 2,), F32)])
    cos, sin = jnp.cos(ang), jnp.sin(ang) * sign
    return jnp.stack([cos * q_scale, cos]), jnp.stack([sin * q_scale, sin])


def _mixer_dilated(xb, wqkv, wo, B, S, D):
    hd = D // N_HEADS
    assert hd == LANES
    cos_tab, sin_tab = _rope_tables(S, hd, hd**-0.5)
    wb = wqkv.astype(BF16)
    qk = _matmul_rope(xb, wb[:, : 2 * D], cos_tab, sin_tab, S)
    v = _matmul(xb, wb[:, 2 * D :], out_dtype=BF16)
    outs, lses = [], []
    for (window, dilation), heads in zip(DILATED_PATTERNS, (4, 4, 8)):
        o, l = _dilated_branch(qk, v, B, S, D, window, dilation, min(heads, N_HEADS))
        outs.append(o)
        lses.append(l)
    o = _merge_branches(outs, lses, N_HEADS)
    return _matmul(o.reshape(B * S, D), wo.astype(BF16), out_dtype=F32)


def _pool_ln_body(x_ref, xh_ref, w_ref, sc_ref, g_ref, b_ref, of_ref, ob_ref, xs_ref, y_ref, *, alpha, ts):
    i = pl.program_id(1)
    x = x_ref[0]
    xs_ref[0:HALO_ROWS, :] = jnp.where(i > 0, xh_ref[0], 0.0)
    xs_ref[HALO_ROWS:, :] = x
    cg = x.shape[1] // len(POOL_WINDOWS)
    t1 = (i * ts + 1 + lax.broadcasted_iota(jnp.int32, (ts, 1), 0)).astype(F32)
    for g, win in enumerate(POOL_WINDOWS):
        cs = slice(g * cg, (g + 1) * cg)
        tot = x[:, cs]
        for j in range(1, win):
            tot = tot + xs_ref[pl.ds(HALO_ROWS - j, ts), cs]
        pooled = tot / jnp.minimum(t1, float(win)) - x[:, cs]
        y = jnp.dot(pooled.astype(BF16), w_ref[g], preferred_element_type=F32)
        y_ref[:, cs] = y * sc_ref[:, cs]
    out = _layer_norm_rows(alpha * x + y_ref[...], g_ref[...], b_ref[...])
    of_ref[0] = out
    ob_ref[0] = out.astype(BF16)


def _mixer_pool_ln(xf, wgrp, scale, g, b, alpha, B, S, D, *, ts=256):
    ts = _tile(S, ts)
    assert ts % HALO_ROWS == 0 and max(POOL_WINDOWS) <= HALO_ROWS
    G, cg, _ = wgrp.shape
    x3 = xf.reshape(B, S, D)
    row = pl.BlockSpec((1, ts, D), lambda bb, i: (bb, i, 0))
    halo = pl.BlockSpec((1, HALO_ROWS, D), lambda bb, i: (bb, jnp.maximum(i * (ts // HALO_ROWS) - 1, 0), 0))
    vec = pl.BlockSpec((1, D), lambda bb, i: (0, 0))
    of, ob = pl.pallas_call(
        functools.partial(_pool_ln_body, alpha=alpha, ts=ts),
        grid=(B, S // ts),
        in_specs=[row, halo, pl.BlockSpec((G, cg, cg), lambda bb, i: (0, 0, 0)), vec, vec, vec],
        out_specs=[row, row],
        out_shape=[jax.ShapeDtypeStruct((B, S, D), F32), jax.ShapeDtypeStruct((B, S, D), BF16)],
        scratch_shapes=[pltpu.VMEM((ts + HALO_ROWS, D), F32), pltpu.VMEM((ts, D), F32)],
        compiler_params=_params("parallel", "parallel"),
        name="pool_ln",
    )(x3, x3, wgrp.astype(BF16), scale.reshape(1, D), g.reshape(1, D), b.reshape(1, D))
    return of.reshape(B * S, D), ob.reshape(B * S, D)


def _fgate_body(z_ref, bf_ref, c_ref):
    z = z_ref[0] + bf_ref[...]
    c = jnp.minimum(z, 0.0) - jnp.log1p(jnp.exp(-jnp.abs(z)))
    S = c.shape[0]
    row = lax.broadcasted_iota(jnp.int32, c.shape, 0)
    shift = 1
    while shift < S:
        c = c + jnp.where(row >= shift, pltpu.roll(c, shift, 0), 0.0)
        shift *= 2
    c_ref[0] = c


def _forget_cumsum(z, bias_row):
    B, S, W = z.shape
    blk = pl.BlockSpec((1, S, W), lambda b: (b, 0, 0))
    return pl.pallas_call(
        _fgate_body,
        grid=(B,),
        in_specs=[blk, pl.BlockSpec((1, W), lambda b: (0, 0))],
        out_specs=blk,
        out_shape=jax.ShapeDtypeStruct((B, S, W), F32),
        compiler_params=_params("parallel"),
        name="forget_cumsum",
    )(z, bias_row)


def _fox_body(q_ref, k_ref, v_ref, cc_ref, cr_ref, o_ref, *, bq, bk):
    h = pl.program_id(1)
    S = q_ref.shape[1]
    diag_steps = bq // bk
    lane = lax.broadcasted_iota(jnp.int32, (bq, LANES), 1)
    key_minus_query = lax.broadcasted_iota(jnp.int32, (bq, bk), 1) - lax.broadcasted_iota(jnp.int32, (bq, bk), 0)
    nt = (((1,), (1,)), ((), ()))
    across = lambda t: jnp.concatenate([t] * (bk // LANES), axis=1)

    def q_block(i, carry):
        r0 = pl.multiple_of(i * bq, bq)
        q = q_ref[0, pl.ds(r0, bq), :]
        cq = jnp.sum(jnp.where(lane == h, cc_ref[0, pl.ds(r0, bq), :], 0.0), axis=1, keepdims=True)
        cq = jnp.broadcast_to(cq, (bq, LANES))

        def k_step(c0, state, diag_offset):
            m, l, acc = state
            s = lax.dot_general(q, k_ref[0, pl.ds(c0, bk), :], nt, preferred_element_type=F32)
            s = s + (across(cq) - cr_ref[0, :, pl.ds(c0, bk)])
            if diag_offset is not None:
                s = jnp.where(key_minus_query <= -diag_offset, s, MASK_VALUE)
            m_new = jnp.maximum(m, jnp.max(s, axis=1, keepdims=True))
            a = jnp.exp(m - m_new)
            p = jnp.exp(s - across(m_new))
            l = a * l + jnp.sum(p, axis=1, keepdims=True)
            acc = a * acc + jnp.dot(p.astype(BF16), v_ref[0, pl.ds(c0, bk), :], preferred_element_type=F32)
            return m_new, l, acc

        state = (jnp.full((bq, LANES), MASK_VALUE, F32), jnp.zeros((bq, LANES), F32), jnp.zeros((bq, LANES), F32))
        state = lax.fori_loop(
            0, i * diag_steps, lambda j, st: k_step(pl.multiple_of(j * bk, bk), st, None), state
        )
        for jj in range(diag_steps):
            state = k_step(pl.multiple_of(r0 + jj * bk, bk), state, jj * bk)
        _, l, acc = state
        o_ref[0, pl.ds(r0, bq), :] = (acc / l).astype(o_ref.dtype)
        return carry

    lax.fori_loop(0, S // bq, q_block, 0)


def _fox_attention(proj, c_col, c_row, B, S, D, *, bq=1024, bk=256):
    bq = _tile(S, bq)
    bk = _tile(bq, bk)
    H = D // LANES
    head = lambda off: pl.BlockSpec((1, S, LANES), lambda b, h: (b, 0, off + h))
    return pl.pallas_call(
        functools.partial(_fox_body, bq=bq, bk=bk),
        grid=(B, H),
        in_specs=[
            head(0),
            head(H),
            head(2 * H),
            pl.BlockSpec((1, S, LANES), lambda b, h: (b, 0, 0)),
            pl.BlockSpec((1, 1, S), lambda b, h: (b * H + h, 0, 0)),
        ],
        out_specs=pl.BlockSpec((1, S, LANES), lambda b, h: (b, 0, h)),
        out_shape=jax.ShapeDtypeStruct((B, S, D), BF16),
        compiler_params=_params("parallel", "arbitrary"),
        name="fox_attention",
    )(proj, proj, proj, c_col, c_row)


def _mixer_fox(xb, win, bf, wo, B, S, D):
    H = N_HEADS
    assert D // H == LANES and H <= LANES
    wb = win.astype(BF16)
    col_scale = jnp.concatenate([jnp.full((D,), LANES**-0.5, F32), jnp.ones((2 * D,), F32)])
    proj = _matmul(xb, wb[:, : 3 * D], out_dtype=BF16, col_scale=col_scale)
    w_gate = jnp.pad(wb[:, 3 * D :], ((0, 0), (0, LANES - H)))
    z = _matmul(xb, w_gate, out_dtype=F32, bn=LANES)
    bias_row = jnp.pad(bf.astype(F32), (0, LANES - H)).reshape(1, LANES)
    c_col = _forget_cumsum(z.reshape(B, S, LANES), bias_row)
    c_row = jnp.swapaxes(c_col[:, :, :H], 1, 2).reshape(B * H, 1, S)
    o = _fox_attention(proj.reshape(B, S, 3 * D), c_col, c_row, B, S, D)
    return _matmul(o.reshape(B * S, D), wo.astype(BF16), out_dtype=F32)


def _convgate_body(b_ref, c_ref, h_ref, ch_ref, hh_ref, w_ref, o_ref, zs_ref, *, ts):
    i = pl.program_id(1)
    z = c_ref[0].astype(F32) * h_ref[0].astype(F32)
    zh = ch_ref[0].astype(F32) * hh_ref[0].astype(F32)
    zs_ref[0:HALO_ROWS, :] = jnp.where(i > 0, zh, 0.0)
    zs_ref[HALO_ROWS:, :] = z
    w = w_ref[...]
    conv = w[CONV_WIDTH - 1 : CONV_WIDTH] * z
    for j in range(1, CONV_WIDTH):
        tap = CONV_WIDTH - 1 - j
        conv = conv + w[tap : tap + 1] * zs_ref[pl.ds(HALO_ROWS - j, ts), :]
    o_ref[0] = (b_ref[0].astype(F32) * conv).astype(o_ref.dtype)


def _conv_gate(proj, wconv, B, S, D, *, ts=512, tc=1024):
    ts, tc = _tile(S, ts), _tile(D, tc)
    assert ts % HALO_ROWS == 0 and CONV_WIDTH - 1 <= HALO_ROWS
    nct = D // tc
    cur = lambda part: pl.BlockSpec((1, ts, tc), lambda b, i, j: (b, i, part * nct + j))
    halo = lambda part: pl.BlockSpec(
        (1, HALO_ROWS, tc), lambda b, i, j: (b, jnp.maximum(i * (ts // HALO_ROWS) - 1, 0), part * nct + j)
    )
    return pl.pallas_call(
        functools.partial(_convgate_body, ts=ts),
        grid=(B, S // ts, nct),
        in_specs=[cur(0), cur(1), cur(2), halo(1), halo(2), pl.BlockSpec((CONV_WIDTH, tc), lambda b, i, j: (0, j))],
        out_specs=pl.BlockSpec((1, ts, tc), lambda b, i, j: (b, i, j)),
        out_shape=jax.ShapeDtypeStruct((B, S, D), BF16),
        scratch_shapes=[pltpu.VMEM((ts + HALO_ROWS, tc), F32)],
        compiler_params=_params("parallel", "parallel", "parallel"),
        name="conv_gate",
    )(proj, proj, proj, proj, proj, wconv.astype(F32))


def _mixer_conv(xb, win, wconv, wout, B, S, D):
    proj = _matmul(xb, win.astype(BF16), out_dtype=BF16)
    u = _conv_gate(proj.reshape(B, S, 3 * D), wconv, B, S, D)
    return _matmul(u.reshape(B * S, D), wout.astype(BF16), out_dtype=F32)


def kernel(x, ln_g, ln_b, mlp_w1, mlp_w2, a_wqkv, a_wo, b_wgrp, b_scale, c_win, c_bf, c_wo, d_win, d_conv, d_wout):
    B, S, D = x.shape
    depth = ln_g.shape[0]
    n_mixers = 4
    alpha = (2.0 * depth) ** 0.25
    xf = x.reshape(B * S, D).astype(F32)
    xb = xf.astype(BF16)
    for i in range(depth):
        kind, j = i % n_mixers, i // n_mixers
        if kind == 1:
            xf, xb = _mixer_pool_ln(xf, b_wgrp[j], b_scale[j], ln_g[i, 0], ln_b[i, 0], alpha, B, S, D)
        else:
            if kind == 0:
                y = _mixer_dilated(xb, a_wqkv[j], a_wo[j], B, S, D)
            elif kind == 2:
                y = _mixer_fox(xb, c_win[j], c_bf[j], c_wo[j], B, S, D)
            else:
                y = _mixer_conv(xb, d_win[j], d_conv[j], d_wout[j], B, S, D)
            xf, xb = _ln_residual(xf, y, ln_g[i, 0], ln_b[i, 0], alpha)
        hid = _matmul(xb, mlp_w1[i].astype(BF16), out_dtype=BF16, act="relu2")
        y = _matmul(hid, mlp_w2[i].astype(BF16), out_dtype=F32, bk=4096)
        xf, xb = _ln_residual(xf, y, ln_g[i, 1], ln_b[i, 1], alpha)
    return xf.reshape(B, S, D).astype(x.dtype)
```

```python
import functools

import jax
import jax.numpy as jnp
from jax import lax
from jax.experimental import pallas as pl
from jax.experimental.pallas import tpu as pltpu

N_HEADS = 32
ROPE_THETA = 10000.0
DILATED_PATTERNS = ((128, 1), (512, 4), (2048, 16))
POOL_WINDOWS = (2, 4, 8, 16)
CONV_WIDTH = 3
LN_EPS = 1e-5
LANES = 128
HALO_ROWS = 16
VMEM_LIMIT_BYTES = 56 * 1024 * 1024
MASK_VALUE = -1e30
LOG2_E = 1.4426950408889634

BF16 = jnp.bfloat16
F32 = jnp.float32


def _tile(n, pref):
    t = min(n, pref)
    while n % t:
        assert t % 2 == 0, (n, pref)
        t //= 2
    return t


def _params(*sem):
    return pltpu.CompilerParams(dimension_semantics=sem, vmem_limit_bytes=VMEM_LIMIT_BYTES)


def _mm_acc_body(x_ref, w_ref, o_ref):
    acc = jnp.dot(x_ref[...], w_ref[...], preferred_element_type=F32)
    k = pl.program_id(2)

    @pl.when(k == 0)
    def _():
        o_ref[...] = acc

    @pl.when(k > 0)
    def _():
        o_ref[...] += acc


def _matmul_bf16w(x, w, layer, *, bm=1024, bn=1024, bk=4096):
    M, K = x.shape
    N = w.shape[2]
    bm, bn, bk = _tile(M, bm), _tile(N, bn), _tile(K, bk)
    return pl.pallas_call(
        _mm_acc_body,
        grid=(M // bm, N // bn, K // bk),
        in_specs=[
            pl.BlockSpec((bm, bk), lambda i, j, k: (i, k)),
            pl.BlockSpec((None, bk, bn), lambda i, j, k: (layer, k, j)),
        ],
        out_specs=pl.BlockSpec((bm, bn), lambda i, j, k: (i, j)),
        out_shape=jax.ShapeDtypeStruct((M, N), F32),
        compiler_params=_params("parallel", "parallel", "arbitrary"),
        name="matmul_ktiled",
    )(x, w)


def _mm_body(*refs, act, has_scale, rope):
    x_ref, w_ref = refs[0], refs[1]
    o_ref, wb_ref = refs[-2], refs[-1]

    @pl.when(pl.program_id(1) == 0)
    def _():
        wb_ref[...] = w_ref[...].astype(BF16)

    acc = jnp.dot(x_ref[...], wb_ref[...], preferred_element_type=F32)
    if rope:
        cos, sin = refs[2][0], refs[3][0]
        for h in range(acc.shape[1] // LANES):
            cs = slice(h * LANES, (h + 1) * LANES)
            a = acc[:, cs]
            o_ref[:, cs] = (a * cos + pltpu.roll(a, LANES // 2, 1) * sin).astype(o_ref.dtype)
        return
    if act == "relu2":
        acc = jnp.maximum(acc, 0.0)
        acc = acc * acc
    if has_scale:
        acc = acc * refs[2][...]
    o_ref[...] = acc.astype(o_ref.dtype)


def _matmul(x, w, layer, *, n_cols, out_dtype, col_offset=0, act=None, col_scale=None, rope=None, bm=1024, bn=512):
    M, K = x.shape
    N = n_cols
    if rope is not None:
        cos_tab, sin_tab, seq = rope
        bm, bn = _tile(seq, bm), _tile(N // 2, bn)
    else:
        bm, bn = _tile(M, bm), _tile(N, bn)
    assert col_offset % bn == 0
    off = col_offset // bn
    in_specs = [
        pl.BlockSpec((bm, K), lambda j, i: (i, 0)),
        pl.BlockSpec((None, K, bn), lambda j, i: (layer, 0, j + off)),
    ]
    args = [x, w]
    if rope is not None:
        tiles_per_part, seq_tiles = (N // 2) // bn, seq // bm
        tab_spec = pl.BlockSpec((1, bm, LANES), lambda j, i: (j // tiles_per_part, i % seq_tiles, 0))
        in_specs += [tab_spec, tab_spec]
        args += [cos_tab, sin_tab]
    elif col_scale is not None:
        in_specs.append(pl.BlockSpec((1, bn), lambda j, i: (0, j)))
        args.append(col_scale.reshape(1, N).astype(F32))
    return pl.pallas_call(
        functools.partial(_mm_body, act=act, has_scale=col_scale is not None, rope=rope is not None),
        grid=(N // bn, M // bm),
        in_specs=in_specs,
        out_specs=pl.BlockSpec((bm, bn), lambda j, i: (i, j)),
        out_shape=jax.ShapeDtypeStruct((M, N), out_dtype),
        scratch_shapes=[pltpu.VMEM((K, bn), BF16)],
        compiler_params=_params("arbitrary", "arbitrary"),
        name="matmul",
    )(*args)


def _layer_norm_rows(v, g, b):
    mu = jnp.mean(v, axis=-1, keepdims=True)
    xc = v - mu
    var = jnp.mean(xc * xc, axis=-1, keepdims=True)
    return xc * lax.rsqrt(var + LN_EPS) * g + b


def _ln_res_body(x_ref, y_ref, g_ref, b_ref, of_ref, ob_ref, *, alpha):
    out = _layer_norm_rows(alpha * x_ref[...] + y_ref[...], g_ref[...], b_ref[...])
    of_ref[...] = out
    ob_ref[...] = out.astype(BF16)


def _ln_residual(x, y, g, b, alpha, *, tm=256):
    M, D = x.shape
    tm = _tile(M, tm)
    row = pl.BlockSpec((tm, D), lambda i: (i, 0))
    vec = pl.BlockSpec((1, D), lambda i: (0, 0))
    return pl.pallas_call(
        functools.partial(_ln_res_body, alpha=alpha),
        grid=(M // tm,),
        in_specs=[row, row, vec, vec],
        out_specs=[row, row],
        out_shape=[jax.ShapeDtypeStruct((M, D), F32), jax.ShapeDtypeStruct((M, D), BF16)],
        compiler_params=_params("parallel"),
        name="ln_residual",
    )(x, y, g.reshape(1, D), b.reshape(1, D))


def _dilated_body(q_ref, k_ref, v_ref, o_ref, lse_ref, *, heads, blk):
    hg = pl.program_id(2)
    n_blocks = q_ref.shape[1] // blk

    @pl.when(hg == 0)
    def _():
        lse_ref[...] = jnp.zeros_like(lse_ref)

    lane = lax.broadcasted_iota(jnp.int32, (blk, LANES), 1)
    qi = lax.broadcasted_iota(jnp.int32, (blk, 2 * blk), 0)
    kj = lax.broadcasted_iota(jnp.int32, (blk, 2 * blk), 1)
    band = (kj >= qi) & (kj <= qi + blk)
    qi1 = lax.broadcasted_iota(jnp.int32, (blk, blk), 0)
    kj1 = lax.broadcasted_iota(jnp.int32, (blk, blk), 1)
    tri = kj1 <= qi1
    nt = (((1,), (1,)), ((), ()))

    def block(rows, keys, mask):
        cols = [slice(hh * LANES, (hh + 1) * LANES) for hh in range(heads)]
        scores = [
            lax.dot_general(q_ref[0, rows, cs], k_ref[0, keys, cs], nt, preferred_element_type=F32) for cs in cols
        ]
        lse_tile = lse_ref[0, rows, :]
        probs, dens = [], []
        for hh, s in enumerate(scores):
            s = jnp.where(mask, s, MASK_VALUE)
            m = jnp.max(s, axis=1, keepdims=True)
            p = jnp.exp(s - m)
            den = jnp.sum(p, axis=1, keepdims=True)
            lse_tile = jnp.where(lane == hg * heads + hh, m + jnp.log(den), lse_tile)
            probs.append(p.astype(BF16))
            dens.append(den)
        lse_ref[0, rows, :] = lse_tile
        for cs, p, den in zip(cols, probs, dens):
            o = jnp.dot(p, v_ref[0, keys, cs], preferred_element_type=F32)
            o_ref[0, rows, cs] = (o / den).astype(o_ref.dtype)

    def later_block(n, carry):
        r0 = pl.multiple_of(n * blk, blk)
        block(pl.ds(r0, blk), pl.ds(r0 - blk, 2 * blk), band)
        return carry

    block(pl.ds(0, blk), pl.ds(0, blk), tri)
    lax.fori_loop(1, n_blocks, later_block, 0)


def _dilated_branch(qk, v, B, S, D, window, dilation, heads):
    blk = window // dilation
    assert S % (blk * dilation) == 0
    sc = S // dilation
    W = heads * LANES
    qk_v = qk.reshape(B, sc, dilation * 2 * D)
    v_v = v.reshape(B, sc, dilation * D)
    qk_tiles, v_tiles = 2 * D // W, D // W
    o, lse = pl.pallas_call(
        functools.partial(_dilated_body, heads=heads, blk=blk),
        grid=(B, dilation, D // W),
        in_specs=[
            pl.BlockSpec((1, sc, W), lambda b, r, g: (b, 0, r * qk_tiles + g)),
            pl.BlockSpec((1, sc, W), lambda b, r, g: (b, 0, r * qk_tiles + v_tiles + g)),
            pl.BlockSpec((1, sc, W), lambda b, r, g: (b, 0, r * v_tiles + g)),
        ],
        out_specs=[
            pl.BlockSpec((1, sc, W), lambda b, r, g: (b, 0, r * v_tiles + g)),
            pl.BlockSpec((1, sc, LANES), lambda b, r, g: (b, 0, r)),
        ],
        out_shape=[
            jax.ShapeDtypeStruct((B, sc, dilation * D), BF16),
            jax.ShapeDtypeStruct((B, sc, dilation * LANES), F32),
        ],
        compiler_params=_params("parallel", "parallel", "arbitrary"),
        name=f"dilated_d{dilation}",
    )(qk_v, qk_v, v_v)
    return o.reshape(B, S, D), lse.reshape(B, S, LANES)


def _merge_body(*refs, n_heads, n_branch):
    o_refs, l_refs, out_ref = refs[:n_branch], refs[n_branch : 2 * n_branch], refs[-1]
    lses = [r[0] for r in l_refs]
    m = functools.reduce(jnp.maximum, lses)
    es = [jnp.exp(l - m) for l in lses]
    tot = functools.reduce(lambda a, b: a + b, es)
    ws = [e / tot for e in es]
    ts = out_ref.shape[1]
    for h in range(n_heads):
        cs = slice(h * LANES, (h + 1) * LANES)
        acc = None
        for w, o_ref in zip(ws, o_refs):
            term = jnp.broadcast_to(w[:, h : h + 1], (ts, LANES)) * o_ref[0, :, cs].astype(F32)
            acc = term if acc is None else acc + term
        out_ref[0, :, cs] = acc.astype(out_ref.dtype)


def _merge_branches(outs, lses, n_heads, *, ts=256):
    B, S, D = outs[0].shape
    ts = _tile(S, ts)
    n = len(outs)
    o_spec = pl.BlockSpec((1, ts, D), lambda b, i: (b, i, 0))
    l_spec = pl.BlockSpec((1, ts, LANES), lambda b, i: (b, i, 0))
    return pl.pallas_call(
        functools.partial(_merge_body, n_heads=n_heads, n_branch=n),
        grid=(B, S // ts),
        in_specs=[o_spec] * n + [l_spec] * n,
        out_specs=o_spec,
        out_shape=jax.ShapeDtypeStruct((B, S, D), BF16),
        compiler_params=_params("parallel", "parallel"),
        name="dilated_merge",
    )(*outs, *lses)


def _rope_tables(S, hd, q_scale):
    pos = jnp.arange(S, dtype=F32)
    inv = ROPE_THETA ** (-jnp.arange(0, hd, 2, dtype=F32) / hd)
    ang = pos[:, None] * inv[None, :]
    ang = jnp.concatenate([ang, ang], axis=-1)
    sign = jnp.concatenate([-jnp.ones((hd // 2,), F32), jnp.ones((hd // 2,), F32)])
    cos, sin = jnp.cos(ang), jnp.sin(ang) * sign
    return jnp.stack([cos * q_scale, cos]), jnp.stack([sin * q_scale, sin])


def _mixer_dilated(xb, wqkv, wo, layer, B, S, D):
    hd = D // N_HEADS
    assert hd == LANES
    cos_tab, sin_tab = _rope_tables(S, hd, hd**-0.5)
    qk = _matmul(xb, wqkv, layer, n_cols=2 * D, out_dtype=BF16, rope=(cos_tab, sin_tab, S))
    v = _matmul(xb, wqkv, layer, n_cols=D, col_offset=2 * D, out_dtype=BF16)
    outs, lses = [], []
    for (window, dilation), heads in zip(DILATED_PATTERNS, (4, 4, 8)):
        o, l = _dilated_branch(qk, v, B, S, D, window, dilation, min(heads, N_HEADS))
        outs.append(o)
        lses.append(l)
    o = _merge_branches(outs, lses, N_HEADS)
    return _matmul(o.reshape(B * S, D), wo, layer, n_cols=D, out_dtype=F32)


def _pool_ln_body(x_ref, xh_ref, w_ref, sc_ref, g_ref, b_ref, of_ref, ob_ref, xs_ref, y_ref, *, alpha, ts):
    i = pl.program_id(1)
    x = x_ref[0]
    xs_ref[0:HALO_ROWS, :] = jnp.where(i > 0, xh_ref[0], 0.0)
    xs_ref[HALO_ROWS:, :] = x
    cg = x.shape[1] // len(POOL_WINDOWS)
    t1 = (i * ts + 1 + lax.broadcasted_iota(jnp.int32, (ts, 1), 0)).astype(F32)
    for g, win in enumerate(POOL_WINDOWS):
        cs = slice(g * cg, (g + 1) * cg)
        tot = x[:, cs]
        for j in range(1, win):
            tot = tot + xs_ref[pl.ds(HALO_ROWS - j, ts), cs]
        pooled = tot / jnp.minimum(t1, float(win)) - x[:, cs]
        y = jnp.dot(pooled.astype(BF16), w_ref[g], preferred_element_type=F32)
        y_ref[:, cs] = y * sc_ref[:, cs]
    out = _layer_norm_rows(alpha * x + y_ref[...], g_ref[...], b_ref[...])
    of_ref[0] = out
    ob_ref[0] = out.astype(BF16)


def _mixer_pool_ln(xf, wgrp, scale, g, b, alpha, B, S, D, *, ts=256):
    ts = _tile(S, ts)
    assert ts % HALO_ROWS == 0 and max(POOL_WINDOWS) <= HALO_ROWS
    G, cg, _ = wgrp.shape
    x3 = xf.reshape(B, S, D)
    row = pl.BlockSpec((1, ts, D), lambda bb, i: (bb, i, 0))
    halo = pl.BlockSpec((1, HALO_ROWS, D), lambda bb, i: (bb, jnp.maximum(i * (ts // HALO_ROWS) - 1, 0), 0))
    vec = pl.BlockSpec((1, D), lambda bb, i: (0, 0))
    of, ob = pl.pallas_call(
        functools.partial(_pool_ln_body, alpha=alpha, ts=ts),
        grid=(B, S // ts),
        in_specs=[row, halo, pl.BlockSpec((G, cg, cg), lambda bb, i: (0, 0, 0)), vec, vec, vec],
        out_specs=[row, row],
        out_shape=[jax.ShapeDtypeStruct((B, S, D), F32), jax.ShapeDtypeStruct((B, S, D), BF16)],
        scratch_shapes=[pltpu.VMEM((ts + HALO_ROWS, D), F32), pltpu.VMEM((ts, D), F32)],
        compiler_params=_params("parallel", "parallel"),
        name="pool_ln",
    )(x3, x3, wgrp.astype(BF16), scale.reshape(1, D), g.reshape(1, D), b.reshape(1, D))
    return of.reshape(B * S, D), ob.reshape(B * S, D)


def _fgate_body(z_ref, bf_ref, c_ref):
    z = z_ref[0] + bf_ref[...]
    c = jnp.minimum(z, 0.0) - jnp.log1p(jnp.exp(-jnp.abs(z)))
    S = c.shape[0]
    row = lax.broadcasted_iota(jnp.int32, c.shape, 0)
    shift = 1
    while shift < S:
        c = c + jnp.where(row >= shift, pltpu.roll(c, shift, 0), 0.0)
        shift *= 2
    c_ref[0] = c


def _forget_cumsum(z, bias_row):
    B, S, W = z.shape
    blk = pl.BlockSpec((1, S, W), lambda b: (b, 0, 0))
    return pl.pallas_call(
        _fgate_body,
        grid=(B,),
        in_specs=[blk, pl.BlockSpec((1, W), lambda b: (0, 0))],
        out_specs=blk,
        out_shape=jax.ShapeDtypeStruct((B, S, W), F32),
        compiler_params=_params("parallel"),
        name="forget_cumsum",
    )(z, bias_row)


def _fox_body(q_ref, k_ref, v_ref, cc_ref, o_ref, qa_ref, ka_ref, *, bq, bk, heads):
    S = q_ref.shape[1]
    diag_steps = bq // bk
    lane = lax.broadcasted_iota(jnp.int32, (S, LANES), 1)
    for t in range(heads):
        head_lane = pl.program_id(1) * heads + t
        c = jnp.sum(jnp.where(lane == head_lane, cc_ref[0], 0.0), axis=1, keepdims=True) * LOG2_E
        hi = c.astype(BF16).astype(F32)
        mid = (c - hi).astype(BF16).astype(F32)
        lo = c - hi - mid

        def pieces(sign, first, hi=hi, mid=mid, lo=lo):
            return (
                jnp.where(lane == first, sign * hi, 0.0)
                + jnp.where(lane == first + 1, sign * mid, 0.0)
                + jnp.where(lane == first + 2, sign * lo, 0.0)
            )

        qa_ref[t] = (pieces(1.0, 0) + jnp.where((lane >= 3) & (lane < 6), 1.0, 0.0)).astype(BF16)
        ka_ref[t] = (jnp.where(lane < 3, 1.0, 0.0) + pieces(-1.0, 3)).astype(BF16)

    key_minus_query = lax.broadcasted_iota(jnp.int32, (bq, bk), 1) - lax.broadcasted_iota(jnp.int32, (bq, bk), 0)
    nt = (((1,), (1,)), ((), ()))
    across = lambda t: jnp.concatenate([t] * (bk // LANES), axis=1)
    cols = [slice(t * LANES, (t + 1) * LANES) for t in range(heads)]

    def q_block(i, carry):
        r0 = pl.multiple_of(i * bq, bq)
        rows = pl.ds(r0, bq)
        qs = [jnp.concatenate([q_ref[0, rows, cs], qa_ref[t, rows, :]], axis=1) for t, cs in enumerate(cols)]

        def k_step(c0, states, diag_offset):
            keys = pl.ds(c0, bk)
            scores = [
                lax.dot_general(
                    q, jnp.concatenate([k_ref[0, keys, cs], ka_ref[t, keys, :]], axis=1), nt, preferred_element_type=F32
                )
                for t, (q, cs) in enumerate(zip(qs, cols))
            ]
            partial = []
            for s, (m, l, acc) in zip(scores, states):
                if diag_offset is not None:
                    s = jnp.where(key_minus_query <= -diag_offset, s, MASK_VALUE)
                m_new = jnp.maximum(m, jnp.max(s, axis=1, keepdims=True))
                a = jnp.exp2(m - m_new)
                p = jnp.exp2(s - across(m_new))
                partial.append((m_new, a * l + jnp.sum(p, axis=1, keepdims=True), a * acc, p.astype(BF16)))
            return tuple(
                (m, l, acc + jnp.dot(p, v_ref[0, keys, cs], preferred_element_type=F32))
                for (m, l, acc, p), cs in zip(partial, cols)
            )

        init = (jnp.full((bq, LANES), MASK_VALUE, F32), jnp.zeros((bq, LANES), F32), jnp.zeros((bq, LANES), F32))
        states = lax.fori_loop(
            0, i * diag_steps, lambda j, st: k_step(pl.multiple_of(j * bk, bk), st, None), (init,) * heads
        )
        for jj in range(diag_steps):
            states = k_step(pl.multiple_of(r0 + jj * bk, bk), states, jj * bk)
        for (_, l, acc), cs in zip(states, cols):
            o_ref[0, rows, cs] = (acc / l).astype(o_ref.dtype)
        return carry

    lax.fori_loop(0, S // bq, q_block, 0)


def _fox_attention(proj, c_col, B, S, D, *, bq=1024, bk=1024, heads=2):
    bq = _tile(S, bq)
    bk = _tile(bq, bk)
    W = heads * LANES
    groups = D // W
    part = lambda p: pl.BlockSpec((1, S, W), lambda b, g: (b, 0, p * groups + g))
    return pl.pallas_call(
        functools.partial(_fox_body, bq=bq, bk=bk, heads=heads),
        grid=(B, groups),
        in_specs=[part(0), part(1), part(2), pl.BlockSpec((1, S, LANES), lambda b, g: (b, 0, 0))],
        out_specs=pl.BlockSpec((1, S, W), lambda b, g: (b, 0, g)),
        out_shape=jax.ShapeDtypeStruct((B, S, D), BF16),
        scratch_shapes=[pltpu.VMEM((heads, S, LANES), BF16), pltpu.VMEM((heads, S, LANES), BF16)],
        compiler_params=_params("parallel", "arbitrary"),
        name="fox_attention",
    )(proj, proj, proj, c_col)


def _mixer_fox(xb, win, bf, wo, layer, B, S, D):
    H = N_HEADS
    assert D // H == LANES and H <= LANES
    col_scale = jnp.concatenate([jnp.full((D,), LANES**-0.5 * LOG2_E, F32), jnp.ones((2 * D,), F32)])
    proj = _matmul(xb, win, layer, n_cols=3 * D, out_dtype=BF16, col_scale=col_scale)
    w_gate = jnp.pad(win[layer, :, 3 * D :], ((0, 0), (0, LANES - H)))[None]
    z = _matmul(xb, w_gate, 0, n_cols=LANES, out_dtype=F32)
    bias_row = jnp.pad(bf[layer].astype(F32), (0, LANES - H)).reshape(1, LANES)
    c_col = _forget_cumsum(z.reshape(B, S, LANES), bias_row)
    o = _fox_attention(proj.reshape(B, S, 3 * D), c_col, B, S, D)
    return _matmul(o.reshape(B * S, D), wo, layer, n_cols=D, out_dtype=F32)


def _convgate_body(b_ref, c_ref, h_ref, ch_ref, hh_ref, w_ref, o_ref, zs_ref, *, ts):
    i = pl.program_id(1)
    z = c_ref[0].astype(F32) * h_ref[0].astype(F32)
    zh = ch_ref[0].astype(F32) * hh_ref[0].astype(F32)
    zs_ref[0:HALO_ROWS, :] = jnp.where(i > 0, zh, 0.0)
    zs_ref[HALO_ROWS:, :] = z
    w = w_ref[...]
    conv = w[CONV_WIDTH - 1 : CONV_WIDTH] * z
    for j in range(1, CONV_WIDTH):
        tap = CONV_WIDTH - 1 - j
        conv = conv + w[tap : tap + 1] * zs_ref[pl.ds(HALO_ROWS - j, ts), :]
    o_ref[0] = (b_ref[0].astype(F32) * conv).astype(o_ref.dtype)


def _conv_gate(proj, wconv, B, S, D, *, ts=512, tc=1024):
    ts, tc = _tile(S, ts), _tile(D, tc)
    assert ts % HALO_ROWS == 0 and CONV_WIDTH - 1 <= HALO_ROWS
    nct = D // tc
    cur = lambda part: pl.BlockSpec((1, ts, tc), lambda b, i, j: (b, i, part * nct + j))
    halo = lambda part: pl.BlockSpec(
        (1, HALO_ROWS, tc), lambda b, i, j: (b, jnp.maximum(i * (ts // HALO_ROWS) - 1, 0), part * nct + j)
    )
    return pl.pallas_call(
        functools.partial(_convgate_body, ts=ts),
        grid=(B, S // ts, nct),
        in_specs=[cur(0), cur(1), cur(2), halo(1), halo(2), pl.BlockSpec((CONV_WIDTH, tc), lambda b, i, j: (0, j))],
        out_specs=pl.BlockSpec((1, ts, tc), lambda b, i, j: (b, i, j)),
        out_shape=jax.ShapeDtypeStruct((B, S, D), BF16),
        scratch_shapes=[pltpu.VMEM((ts + HALO_ROWS, tc), F32)],
        compiler_params=_params("parallel", "parallel", "parallel"),
        name="conv_gate",
    )(proj, proj, proj, proj, proj, wconv.astype(F32))


def _mixer_conv(xb, win, wconv, wout, layer, B, S, D):
    proj = _matmul(xb, win, layer, n_cols=3 * D, out_dtype=BF16)
    u = _conv_gate(proj.reshape(B, S, 3 * D), wconv[layer], B, S, D)
    return _matmul(u.reshape(B * S, D), wout, layer, n_cols=D, out_dtype=F32)


def kernel(x, ln_g, ln_b, mlp_w1, mlp_w2, a_wqkv, a_wo, b_wgrp, b_scale, c_win, c_bf, c_wo, d_win, d_conv, d_wout):
    B, S, D = x.shape
    depth = ln_g.shape[0]
    n_mixers = 4
    alpha = (2.0 * depth) ** 0.25
    xf = x.reshape(B * S, D).astype(F32)
    xb = xf.astype(BF16)
    w2b = mlp_w2.astype(BF16)
    for i in range(depth):
        kind, j = i % n_mixers, i // n_mixers
        if kind == 1:
            xf, xb = _mixer_pool_ln(xf, b_wgrp[j], b_scale[j], ln_g[i, 0], ln_b[i, 0], alpha, B, S, D)
        else:
            if kind == 0:
                y = _mixer_dilated(xb, a_wqkv, a_wo, j, B, S, D)
            elif kind == 2:
                y = _mixer_fox(xb, c_win, c_bf, c_wo, j, B, S, D)
            else:
                y = _mixer_conv(xb, d_win, d_conv, d_wout, j, B, S, D)
            xf, xb = _ln_residual(xf, y, ln_g[i, 0], ln_b[i, 0], alpha)
        hid = _matmul(xb, mlp_w1, i, n_cols=mlp_w1.shape[2], out_dtype=BF16, act="relu2")
        y = _matmul_bf16w(hid, w2b, i)
        xf, xb = _ln_residual(xf, y, ln_g[i, 1], ln_b[i, 1], alpha)
    return xf.reshape(B, S, D).astype(x.dtype)
```

```python
import functools

import jax
import jax.numpy as jnp
from jax import lax
from jax.experimental import pallas as pl
from jax.experimental.pallas import tpu as pltpu

N_HEADS = 32
ROPE_THETA = 10000.0
DILATED_PATTERNS = ((128, 1), (512, 4), (2048, 16))
POOL_WINDOWS = (2, 4, 8, 16)
CONV_WIDTH = 3
LN_EPS = 1e-5
LANES = 128
HALO_ROWS = 16
VMEM_LIMIT_BYTES = 56 * 1024 * 1024
VMEM_LIMIT_LARGE_BYTES = 60000 * 1024
MASK_VALUE = -1e30
LOG2_E = 1.4426950408889634

BF16 = jnp.bfloat16
F32 = jnp.float32


def _tile(n, pref):
    t = min(n, pref)
    while n % t:
        assert t % 2 == 0, (n, pref)
        t //= 2
    return t


def _params(*sem, vmem_limit_bytes=VMEM_LIMIT_BYTES):
    return pltpu.CompilerParams(dimension_semantics=sem, vmem_limit_bytes=vmem_limit_bytes)


def _mm_acc_body(x_ref, w_ref, r_ref, o_ref, *, alpha):
    acc = jnp.dot(x_ref[...], w_ref[...], preferred_element_type=F32)
    k = pl.program_id(2)

    @pl.when(k == 0)
    def _():
        o_ref[...] = alpha * r_ref[...] + acc

    @pl.when(k > 0)
    def _():
        o_ref[...] += acc


def _matmul_bf16w(x, w, layer, residual, alpha, *, bm=1024, bn=1024, bk=4096):
    M, K = x.shape
    N = w.shape[2]
    bm, bn, bk = _tile(M, bm), _tile(N, bn), _tile(K, bk)
    out_spec = pl.BlockSpec((bm, bn), lambda i, j, k: (i, j))
    return pl.pallas_call(
        functools.partial(_mm_acc_body, alpha=alpha),
        grid=(M // bm, N // bn, K // bk),
        in_specs=[
            pl.BlockSpec((bm, bk), lambda i, j, k: (i, k)),
            pl.BlockSpec((None, bk, bn), lambda i, j, k: (layer, k, j)),
            out_spec,
        ],
        out_specs=out_spec,
        out_shape=jax.ShapeDtypeStruct((M, N), F32),
        compiler_params=_params("parallel", "parallel", "arbitrary", vmem_limit_bytes=VMEM_LIMIT_LARGE_BYTES),
        name="matmul_ktiled",
    )(x, w, residual)


def _mm_body(*refs, act, has_scale, rope, alpha):
    x_ref, w_ref = refs[0], refs[1]
    o_ref, wb_ref = refs[-2], refs[-1]

    @pl.when(pl.program_id(1) == 0)
    def _():
        wb_ref[...] = w_ref[...].astype(BF16)

    acc = jnp.dot(x_ref[...], wb_ref[...], preferred_element_type=F32)
    if rope:
        cos, sin = refs[2][0], refs[3][0]
        for h in range(acc.shape[1] // LANES):
            cs = slice(h * LANES, (h + 1) * LANES)
            a = acc[:, cs]
            o_ref[:, cs] = (a * cos + pltpu.roll(a, LANES // 2, 1) * sin).astype(o_ref.dtype)
        return
    if act == "relu2":
        acc = jnp.maximum(acc, 0.0)
        acc = acc * acc
    if has_scale:
        acc = acc * refs[2][...]
    if alpha is not None:
        acc = alpha * refs[-3][...] + acc
    o_ref[...] = acc.astype(o_ref.dtype)


def _matmul(x, w, layer, *, n_cols, out_dtype, col_offset=0, act=None, col_scale=None, rope=None, residual=None,
            alpha=None, bm=1024, bn=512):
    M, K = x.shape
    N = n_cols
    if rope is not None:
        cos_tab, sin_tab, seq = rope
        bm, bn = _tile(seq, bm), _tile(N // 2, bn)
    else:
        bm, bn = _tile(M, bm), _tile(N, bn)
    assert col_offset % bn == 0 and (residual is None) == (alpha is None)
    off = col_offset // bn
    out_spec = pl.BlockSpec((bm, bn), lambda j, i: (i, j))
    in_specs = [
        pl.BlockSpec((bm, K), lambda j, i: (i, 0)),
        pl.BlockSpec((None, K, bn), lambda j, i: (layer, 0, j + off)),
    ]
    args = [x, w]
    if rope is not None:
        tiles_per_part, seq_tiles = (N // 2) // bn, seq // bm
        tab_spec = pl.BlockSpec((1, bm, LANES), lambda j, i: (j // tiles_per_part, i % seq_tiles, 0))
        in_specs += [tab_spec, tab_spec]
        args += [cos_tab, sin_tab]
    elif col_scale is not None:
        in_specs.append(pl.BlockSpec((1, bn), lambda j, i: (0, j)))
        args.append(col_scale.reshape(1, N).astype(F32))
    if residual is not None:
        in_specs.append(out_spec)
        args.append(residual)
    return pl.pallas_call(
        functools.partial(_mm_body, act=act, has_scale=col_scale is not None, rope=rope is not None, alpha=alpha),
        grid=(N // bn, M // bm),
        in_specs=in_specs,
        out_specs=out_spec,
        out_shape=jax.ShapeDtypeStruct((M, N), out_dtype),
        scratch_shapes=[pltpu.VMEM((K, bn), BF16)],
        compiler_params=_params("arbitrary", "arbitrary"),
        name="matmul",
    )(*args)


def _layer_norm_rows(v, g, b):
    mu = jnp.mean(v, axis=-1, keepdims=True)
    xc = v - mu
    var = jnp.mean(xc * xc, axis=-1, keepdims=True)
    return xc * lax.rsqrt(var + LN_EPS) * g + b


def _ln_body(v_ref, g_ref, b_ref, of_ref, ob_ref):
    out = _layer_norm_rows(v_ref[...], g_ref[...], b_ref[...])
    of_ref[...] = out
    ob_ref[...] = out.astype(BF16)


def _layer_norm(v, g, b, *, tm=256):
    M, D = v.shape
    tm = _tile(M, tm)
    row = pl.BlockSpec((tm, D), lambda i: (i, 0))
    vec = pl.BlockSpec((1, D), lambda i: (0, 0))
    return pl.pallas_call(
        _ln_body,
        grid=(M // tm,),
        in_specs=[row, vec, vec],
        out_specs=[row, row],
        out_shape=[jax.ShapeDtypeStruct((M, D), F32), jax.ShapeDtypeStruct((M, D), BF16)],
        compiler_params=_params("parallel"),
        name="layer_norm",
    )(v, g.reshape(1, D), b.reshape(1, D))


def _dilated_fused_body(q_ref, k_ref, v_ref, o_ref, q32, k32, v32, qc, kc, vc, ob, lb, *, patterns, unroll):
    S = q_ref.shape[1]
    q32[...] = q_ref[0].astype(F32)
    k32[...] = k_ref[0].astype(F32)
    v32[...] = v_ref[0].astype(F32)
    nt = (((1,), (1,)), ((), ()))
    for g, (window, d) in enumerate(patterns):
        blk = window // d
        sc = S // d
        n_blocks = sc // blk
        per_iter = min(n_blocks, unroll)
        classes = min(d, max(1, unroll // n_blocks))
        pad = sc + blk
        for c in range(classes):
            kc[c * pad : c * pad + blk, :] = jnp.zeros((blk, LANES), BF16)
            vc[c * pad : c * pad + blk, :] = jnp.zeros((blk, LANES), BF16)
        qi = lax.broadcasted_iota(jnp.int32, (blk, 2 * blk), 0)
        kj = lax.broadcasted_iota(jnp.int32, (blk, 2 * blk), 1)

        def class_body(rr, carry, g=g, d=d, blk=blk, sc=sc, n_blocks=n_blocks, per_iter=per_iter, classes=classes,
                       pad=pad, qi=qi, kj=kj):
            for c in range(classes):
                if d == 1:
                    qc[0:sc, :] = q_ref[0]
                    kc[blk:pad, :] = k_ref[0]
                    vc[blk:pad, :] = v_ref[0]
                else:
                    rows = pl.ds(rr * classes + c, sc, stride=d)
                    qc[c * sc : (c + 1) * sc, :] = q32[rows, :].astype(BF16)
                    kc[c * pad + blk : (c + 1) * pad, :] = k32[rows, :].astype(BF16)
                    vc[c * pad + blk : (c + 1) * pad, :] = v32[rows, :].astype(BF16)

            def blocks(it, carry2):
                units = [
                    (c, pl.multiple_of((it * per_iter + u) * blk, blk)) for c in range(classes) for u in range(per_iter)
                ]
                scores = [
                    lax.dot_general(
                        qc[pl.ds(c * sc + n0, blk), :],
                        kc[pl.ds(c * pad + n0, 2 * blk), :],
                        nt,
                        preferred_element_type=F32,
                    )
                    for c, n0 in units
                ]
                probs, dens, lses = [], [], []
                for (c, n0), s in zip(units, scores):
                    visible = (kj >= jnp.maximum(qi, blk - n0)) & (kj <= qi + blk)
                    s = jnp.where(visible, s, MASK_VALUE)
                    m = jnp.max(s, axis=1, keepdims=True)
                    p = jnp.exp(s - m)
                    den = jnp.sum(p, axis=1, keepdims=True)
                    probs.append(p.astype(BF16))
                    dens.append(den)
                    lses.append(m + jnp.log(den))
                for (c, n0), p, den, lse in zip(units, probs, dens, lses):
                    o = jnp.dot(p, vc[pl.ds(c * pad + n0, 2 * blk), :], preferred_element_type=F32) / den
                    out_rows = pl.ds(rr * classes + c + n0 * d, blk, stride=d) if d > 1 else pl.ds(n0, blk)
                    ob[g, out_rows, :] = o
                    lb[g, out_rows, :] = jnp.broadcast_to(lse, (blk, LANES))
                return carry2

            lax.fori_loop(0, n_blocks // per_iter, blocks, 0)
            return carry

        lax.fori_loop(0, d // classes, class_body, 0)

    merge_rows = min(S, 512)

    def merge(t, carry):
        rows = pl.ds(pl.multiple_of(t * merge_rows, merge_rows), merge_rows)
        lses = [lb[g, rows, :] for g in range(len(patterns))]
        m = functools.reduce(jnp.maximum, lses)
        es = [jnp.exp(l - m) for l in lses]
        tot = functools.reduce(lambda a, b: a + b, es)
        acc = functools.reduce(lambda a, b: a + b, [e * ob[g, rows, :] for g, e in enumerate(es)])
        o_ref[0, rows, :] = (acc / tot).astype(o_ref.dtype)
        return carry

    lax.fori_loop(0, S // merge_rows, merge, 0)


def _dilated_attention(qk, v, B, S, D, patterns, *, unroll=4):
    H = D // LANES
    for window, d in patterns:
        assert S % window == 0
    max_blk = max(w // d for w, d in patterns)
    head = lambda off: pl.BlockSpec((1, S, LANES), lambda b, h: (b, 0, off + h))
    return pl.pallas_call(
        functools.partial(_dilated_fused_body, patterns=patterns, unroll=unroll),
        grid=(B, H),
        in_specs=[head(0), head(H), head(0)],
        out_specs=head(0),
        out_shape=jax.ShapeDtypeStruct((B, S, D), BF16),
        scratch_shapes=[
            pltpu.VMEM((S, LANES), F32),
            pltpu.VMEM((S, LANES), F32),
            pltpu.VMEM((S, LANES), F32),
            pltpu.VMEM((S, LANES), BF16),
            pltpu.VMEM((S + unroll * max_blk, LANES), BF16),
            pltpu.VMEM((S + unroll * max_blk, LANES), BF16),
            pltpu.VMEM((len(patterns), S, LANES), F32),
            pltpu.VMEM((len(patterns), S, LANES), F32),
        ],
        compiler_params=_params("parallel", "parallel"),
        name="dilated_attention",
    )(qk.reshape(B, S, 2 * D), qk.reshape(B, S, 2 * D), v.reshape(B, S, D))


def _rope_tables(S, hd, q_scale):
    pos = jnp.arange(S, dtype=F32)
    inv = ROPE_THETA ** (-jnp.arange(0, hd, 2, dtype=F32) / hd)
    ang = pos[:, None] * inv[None, :]
    ang = jnp.concatenate([ang, ang], axis=-1)
    sign = jnp.concatenate([-jnp.ones((hd // 2,), F32), jnp.ones((hd // 2,), F32)])
    cos, sin = jnp.cos(ang), jnp.sin(ang) * sign
    return jnp.stack([cos * q_scale, cos]), jnp.stack([sin * q_scale, sin])


def _mixer_dilated(xf, xb, wqkv, wo, layer, alpha, B, S, D):
    hd = D // N_HEADS
    assert hd == LANES
    cos_tab, sin_tab = _rope_tables(S, hd, hd**-0.5)
    qk = _matmul(xb, wqkv, layer, n_cols=2 * D, out_dtype=BF16, rope=(cos_tab, sin_tab, S))
    v = _matmul(xb, wqkv, layer, n_cols=D, col_offset=2 * D, out_dtype=BF16)
    o = _dilated_attention(qk, v, B, S, D, DILATED_PATTERNS)
    return _matmul(o.reshape(B * S, D), wo, layer, n_cols=D, out_dtype=F32, residual=xf, alpha=alpha)


def _pool_ln_body(x_ref, xh_ref, w_ref, sc_ref, g_ref, b_ref, of_ref, ob_ref, xs_ref, y_ref, *, alpha, ts):
    i = pl.program_id(1)
    x = x_ref[0]
    xs_ref[0:HALO_ROWS, :] = jnp.where(i > 0, xh_ref[0], 0.0)
    xs_ref[HALO_ROWS:, :] = x
    cg = x.shape[1] // len(POOL_WINDOWS)
    t1 = (i * ts + 1 + lax.broadcasted_iota(jnp.int32, (ts, 1), 0)).astype(F32)
    for g, win in enumerate(POOL_WINDOWS):
        cs = slice(g * cg, (g + 1) * cg)
        tot = x[:, cs]
        for j in range(1, win):
            tot = tot + xs_ref[pl.ds(HALO_ROWS - j, ts), cs]
        pooled = tot / jnp.minimum(t1, float(win)) - x[:, cs]
        y = jnp.dot(pooled.astype(BF16), w_ref[g], preferred_element_type=F32)
        y_ref[:, cs] = y * sc_ref[:, cs]
    out = _layer_norm_rows(alpha * x + y_ref[...], g_ref[...], b_ref[...])
    of_ref[0] = out
    ob_ref[0] = out.astype(BF16)


def _mixer_pool_ln(xf, wgrp, scale, g, b, alpha, B, S, D, *, ts=256):
    ts = _tile(S, ts)
    assert ts % HALO_ROWS == 0 and max(POOL_WINDOWS) <= HALO_ROWS
    G, cg, _ = wgrp.shape
    x3 = xf.reshape(B, S, D)
    row = pl.BlockSpec((1, ts, D), lambda bb, i: (bb, i, 0))
    halo = pl.BlockSpec((1, HALO_ROWS, D), lambda bb, i: (bb, jnp.maximum(i * (ts // HALO_ROWS) - 1, 0), 0))
    vec = pl.BlockSpec((1, D), lambda bb, i: (0, 0))
    of, ob = pl.pallas_call(
        functools.partial(_pool_ln_body, alpha=alpha, ts=ts),
        grid=(B, S // ts),
        in_specs=[row, halo, pl.BlockSpec((G, cg, cg), lambda bb, i: (0, 0, 0)), vec, vec, vec],
        out_specs=[row, row],
        out_shape=[jax.ShapeDtypeStruct((B, S, D), F32), jax.ShapeDtypeStruct((B, S, D), BF16)],
        scratch_shapes=[pltpu.VMEM((ts + HALO_ROWS, D), F32), pltpu.VMEM((ts, D), F32)],
        compiler_params=_params("parallel", "parallel"),
        name="pool_ln",
    )(x3, x3, wgrp.astype(BF16), scale.reshape(1, D), g.reshape(1, D), b.reshape(1, D))
    return of.reshape(B * S, D), ob.reshape(B * S, D)


def _fgate_body(z_ref, bf_ref, c_ref):
    z = z_ref[0] + bf_ref[...]
    c = jnp.minimum(z, 0.0) - jnp.log1p(jnp.exp(-jnp.abs(z)))
    S = c.shape[0]
    row = lax.broadcasted_iota(jnp.int32, c.shape, 0)
    shift = 1
    while shift < S:
        c = c + jnp.where(row >= shift, pltpu.roll(c, shift, 0), 0.0)
        shift *= 2
    c_ref[0] = c


def _forget_cumsum(z, bias_row):
    B, S, W = z.shape
    blk = pl.BlockSpec((1, S, W), lambda b: (b, 0, 0))
    return pl.pallas_call(
        _fgate_body,
        grid=(B,),
        in_specs=[blk, pl.BlockSpec((1, W), lambda b: (0, 0))],
        out_specs=blk,
        out_shape=jax.ShapeDtypeStruct((B, S, W), F32),
        compiler_params=_params("parallel"),
        name="forget_cumsum",
    )(z, bias_row)


def _fox_body(q_ref, k_ref, v_ref, cc_ref, o_ref, qa_ref, ka_ref, *, bq, bk, heads):
    S = q_ref.shape[1]
    diag_steps = bq // bk
    lane = lax.broadcasted_iota(jnp.int32, (S, LANES), 1)
    for t in range(heads):
        head_lane = pl.program_id(1) * heads + t
        c = jnp.sum(jnp.where(lane == head_lane, cc_ref[0], 0.0), axis=1, keepdims=True) * LOG2_E
        hi = c.astype(BF16).astype(F32)
        mid = (c - hi).astype(BF16).astype(F32)
        lo = c - hi - mid

        def pieces(sign, first, hi=hi, mid=mid, lo=lo):
            return (
                jnp.where(lane == first, sign * hi, 0.0)
                + jnp.where(lane == first + 1, sign * mid, 0.0)
                + jnp.where(lane == first + 2, sign * lo, 0.0)
            )

        qa_ref[t] = (pieces(1.0, 0) + jnp.where((lane >= 3) & (lane < 6), 1.0, 0.0)).astype(BF16)
        ka_ref[t] = (jnp.where(lane < 3, 1.0, 0.0) + pieces(-1.0, 3)).astype(BF16)

    key_minus_query = lax.broadcasted_iota(jnp.int32, (bq, bk), 1) - lax.broadcasted_iota(jnp.int32, (bq, bk), 0)
    nt = (((1,), (1,)), ((), ()))
    across = lambda t: jnp.concatenate([t] * (bk // LANES), axis=1)
    cols = [slice(t * LANES, (t + 1) * LANES) for t in range(heads)]

    def q_block(i, carry):
        r0 = pl.multiple_of(i * bq, bq)
        rows = pl.ds(r0, bq)
        qs = [jnp.concatenate([q_ref[0, rows, cs], qa_ref[t, rows, :]], axis=1) for t, cs in enumerate(cols)]

        def k_step(c0, states, diag_offset):
            keys = pl.ds(c0, bk)
            scores = [
                lax.dot_general(
                    q, jnp.concatenate([k_ref[0, keys, cs], ka_ref[t, keys, :]], axis=1), nt, preferred_element_type=F32
                )
                for t, (q, cs) in enumerate(zip(qs, cols))
            ]
            partial = []
            for s, (m, l, acc) in zip(scores, states):
                if diag_offset is not None:
                    s = jnp.where(key_minus_query <= -diag_offset, s, MASK_VALUE)
                m_new = jnp.maximum(m, jnp.max(s, axis=1, keepdims=True))
                a = jnp.exp2(m - m_new)
                p = jnp.exp2(s - across(m_new))
                partial.append((m_new, a * l + jnp.sum(p, axis=1, keepdims=True), a * acc, p.astype(BF16)))
            return tuple(
                (m, l, acc + jnp.dot(p, v_ref[0, keys, cs], preferred_element_type=F32))
                for (m, l, acc, p), cs in zip(partial, cols)
            )

        init = (jnp.full((bq, LANES), MASK_VALUE, F32), jnp.zeros((bq, LANES), F32), jnp.zeros((bq, LANES), F32))
        states = lax.fori_loop(
            0, i * diag_steps, lambda j, st: k_step(pl.multiple_of(j * bk, bk), st, None), (init,) * heads
        )
        for jj in range(diag_steps):
            states = k_step(pl.multiple_of(r0 + jj * bk, bk), states, jj * bk)
        for (_, l, acc), cs in zip(states, cols):
            o_ref[0, rows, cs] = (acc / l).astype(o_ref.dtype)
        return carry

    lax.fori_loop(0, S // bq, q_block, 0)


def _fox_attention(proj, c_col, B, S, D, *, bq=1024, bk=1024, heads=2):
    bq = _tile(S, bq)
    bk = _tile(bq, bk)
    W = heads * LANES
    groups = D // W
    part = lambda p: pl.BlockSpec((1, S, W), lambda b, g: (b, 0, p * groups + g))
    return pl.pallas_call(
        functools.partial(_fox_body, bq=bq, bk=bk, heads=heads),
        grid=(B, groups),
        in_specs=[part(0), part(1), part(2), pl.BlockSpec((1, S, LANES), lambda b, g: (b, 0, 0))],
        out_specs=pl.BlockSpec((1, S, W), lambda b, g: (b, 0, g)),
        out_shape=jax.ShapeDtypeStruct((B, S, D), BF16),
        scratch_shapes=[pltpu.VMEM((heads, S, LANES), BF16), pltpu.VMEM((heads, S, LANES), BF16)],
        compiler_params=_params("parallel", "arbitrary"),
        name="fox_attention",
    )(proj, proj, proj, c_col)


def _mixer_fox(xf, xb, win, bf, wo, layer, alpha, B, S, D):
    H = N_HEADS
    assert D // H == LANES and H <= LANES
    col_scale = jnp.concatenate([jnp.full((D,), LANES**-0.5 * LOG2_E, F32), jnp.ones((2 * D,), F32)])
    proj = _matmul(xb, win, layer, n_cols=3 * D, out_dtype=BF16, col_scale=col_scale)
    w_gate = jnp.pad(win[layer, :, 3 * D :], ((0, 0), (0, LANES - H)))[None]
    z = _matmul(xb, w_gate, 0, n_cols=LANES, out_dtype=F32)
    bias_row = jnp.pad(bf[layer].astype(F32), (0, LANES - H)).reshape(1, LANES)
    c_col = _forget_cumsum(z.reshape(B, S, LANES), bias_row)
    o = _fox_attention(proj.reshape(B, S, 3 * D), c_col, B, S, D)
    return _matmul(o.reshape(B * S, D), wo, layer, n_cols=D, out_dtype=F32, residual=xf, alpha=alpha)


def _convgate_body(b_ref, c_ref, h_ref, ch_ref, hh_ref, w_ref, o_ref, zs_ref, *, ts):
    i = pl.program_id(1)
    z = c_ref[0].astype(F32) * h_ref[0].astype(F32)
    zh = ch_ref[0].astype(F32) * hh_ref[0].astype(F32)
    zs_ref[0:HALO_ROWS, :] = jnp.where(i > 0, zh, 0.0)
    zs_ref[HALO_ROWS:, :] = z
    w = w_ref[...]
    conv = w[CONV_WIDTH - 1 : CONV_WIDTH] * z
    for j in range(1, CONV_WIDTH):
        tap = CONV_WIDTH - 1 - j
        conv = conv + w[tap : tap + 1] * zs_ref[pl.ds(HALO_ROWS - j, ts), :]
    o_ref[0] = (b_ref[0].astype(F32) * conv).astype(o_ref.dtype)


def _conv_gate(proj, wconv, B, S, D, *, ts=512, tc=1024):
    ts, tc = _tile(S, ts), _tile(D, tc)
    assert ts % HALO_ROWS == 0 and CONV_WIDTH - 1 <= HALO_ROWS
    nct = D // tc
    cur = lambda part: pl.BlockSpec((1, ts, tc), lambda b, i, j: (b, i, part * nct + j))
    halo = lambda part: pl.BlockSpec(
        (1, HALO_ROWS, tc), lambda b, i, j: (b, jnp.maximum(i * (ts // HALO_ROWS) - 1, 0), part * nct + j)
    )
    return pl.pallas_call(
        functools.partial(_convgate_body, ts=ts),
        grid=(B, S // ts, nct),
        in_specs=[cur(0), cur(1), cur(2), halo(1), halo(2), pl.BlockSpec((CONV_WIDTH, tc), lambda b, i, j: (0, j))],
        out_specs=pl.BlockSpec((1, ts, tc), lambda b, i, j: (b, i, j)),
        out_shape=jax.ShapeDtypeStruct((B, S, D), BF16),
        scratch_shapes=[pltpu.VMEM((ts + HALO_ROWS, tc), F32)],
        compiler_params=_params("parallel", "parallel", "parallel"),
        name="conv_gate",
    )(proj, proj, proj, proj, proj, wconv.astype(F32))


def _mixer_conv(xf, xb, win, wconv, wout, layer, alpha, B, S, D):
    proj = _matmul(xb, win, layer, n_cols=3 * D, out_dtype=BF16)
    u = _conv_gate(proj.reshape(B, S, 3 * D), wconv[layer], B, S, D)
    return _matmul(u.reshape(B * S, D), wout, layer, n_cols=D, out_dtype=F32, residual=xf, alpha=alpha)


def kernel(x, ln_g, ln_b, mlp_w1, mlp_w2, a_wqkv, a_wo, b_wgrp, b_scale, c_win, c_bf, c_wo, d_win, d_conv, d_wout):
    B, S, D = x.shape
    depth = ln_g.shape[0]
    n_mixers = 4
    alpha = (2.0 * depth) ** 0.25
    xf = x.reshape(B * S, D).astype(F32)
    xb = xf.astype(BF16)
    w2b = mlp_w2.astype(BF16)
    for i in range(depth):
        kind, j = i % n_mixers, i // n_mixers
        if kind == 1:
            xf, xb = _mixer_pool_ln(xf, b_wgrp[j], b_scale[j], ln_g[i, 0], ln_b[i, 0], alpha, B, S, D)
        else:
            if kind == 0:
                v = _mixer_dilated(xf, xb, a_wqkv, a_wo, j, alpha, B, S, D)
            elif kind == 2:
                v = _mixer_fox(xf, xb, c_win, c_bf, c_wo, j, alpha, B, S, D)
            else:
                v = _mixer_conv(xf, xb, d_win, d_conv, d_wout, j, alpha, B, S, D)
            xf, xb = _layer_norm(v, ln_g[i, 0], ln_b[i, 0])
        hid = _matmul(xb, mlp_w1, i, n_cols=mlp_w1.shape[2], out_dtype=BF16, act="relu2")
        v = _matmul_bf16w(hid, w2b, i, xf, alpha)
        xf, xb = _layer_norm(v, ln_g[i, 1], ln_b[i, 1])
    return xf.reshape(B, S, D).astype(x.dtype)
```

```python
import functools

import jax
import jax.numpy as jnp
from jax import lax
from jax.experimental import pallas as pl
from jax.experimental.pallas import tpu as pltpu

N_HEADS = 32
ROPE_THETA = 10000.0
DILATED_PATTERNS = ((128, 1), (512, 4), (2048, 16))
POOL_WINDOWS = (2, 4, 8, 16)
CONV_WIDTH = 3
LN_EPS = 1e-5
LANES = 128
HALO_ROWS = 16
VMEM_LIMIT_BYTES = 56 * 1024 * 1024
VMEM_LIMIT_LARGE_BYTES = 60000 * 1024
MASK_VALUE = -1e30
LOG2_E = 1.4426950408889634

BF16 = jnp.bfloat16
F32 = jnp.float32


def _tile(n, pref):
    t = min(n, pref)
    while n % t:
        assert t % 2 == 0, (n, pref)
        t //= 2
    return t


def _params(*sem, vmem_limit_bytes=VMEM_LIMIT_BYTES):
    return pltpu.CompilerParams(dimension_semantics=sem, vmem_limit_bytes=vmem_limit_bytes)


def _mm_acc_body(x_ref, w_ref, r_ref, o_ref, *, alpha):
    acc = jnp.dot(x_ref[...], w_ref[...], preferred_element_type=F32)
    k = pl.program_id(2)

    @pl.when(k == 0)
    def _():
        o_ref[...] = alpha * r_ref[...] + acc

    @pl.when(k > 0)
    def _():
        o_ref[...] += acc


def _matmul_bf16w(x, w, layer, residual, alpha, *, bm=1024, bn=1024, bk=4096):
    M, K = x.shape
    N = w.shape[2]
    bm, bn, bk = _tile(M, bm), _tile(N, bn), _tile(K, bk)
    out_spec = pl.BlockSpec((bm, bn), lambda i, j, k: (i, j))
    return pl.pallas_call(
        functools.partial(_mm_acc_body, alpha=alpha),
        grid=(M // bm, N // bn, K // bk),
        in_specs=[
            pl.BlockSpec((bm, bk), lambda i, j, k: (i, k)),
            pl.BlockSpec((None, bk, bn), lambda i, j, k: (layer, k, j)),
            out_spec,
        ],
        out_specs=out_spec,
        out_shape=jax.ShapeDtypeStruct((M, N), F32),
        compiler_params=_params("parallel", "parallel", "arbitrary", vmem_limit_bytes=VMEM_LIMIT_LARGE_BYTES),
        name="matmul_ktiled",
    )(x, w, residual)


def _mm_body(*refs, act, has_scale, rope, alpha, side):
    x_ref, w_ref = refs[0], refs[1]
    n_extra = 2 if rope else (1 if has_scale else 0)
    n_in = 2 + n_extra + (alpha is not None) + side
    o_ref, wb_ref = refs[n_in], refs[-1]
    if side:
        refs[n_in + 1][...] = refs[n_in - 1][...].astype(BF16)

    @pl.when(pl.program_id(1) == 0)
    def _():
        wb_ref[...] = w_ref[...].astype(BF16)

    acc = jnp.dot(x_ref[...], wb_ref[...], preferred_element_type=F32)
    if rope:
        cos, sin = refs[2][0], refs[3][0]
        for h in range(acc.shape[1] // LANES):
            cs = slice(h * LANES, (h + 1) * LANES)
            a = acc[:, cs]
            o_ref[:, cs] = (a * cos + pltpu.roll(a, LANES // 2, 1) * sin).astype(o_ref.dtype)
        return
    if act == "relu2":
        acc = jnp.maximum(acc, 0.0)
        acc = acc * acc
    if has_scale:
        acc = acc * refs[2][...]
    if alpha is not None:
        acc = alpha * refs[2 + n_extra][...] + acc
    o_ref[...] = acc.astype(o_ref.dtype)


def _matmul(x, w, layer, *, n_cols, out_dtype, col_offset=0, act=None, col_scale=None, rope=None, residual=None,
            alpha=None, narrow=None, bm=1024, bn=512):
    M, K = x.shape
    N = n_cols
    if rope is not None:
        cos_tab, sin_tab, seq = rope
        bm, bn = _tile(seq, bm), _tile(N // 2, bn)
    else:
        bm, bn = _tile(M, bm), _tile(N, bn)
    assert col_offset % bn == 0 and (residual is None) == (alpha is None)
    off = col_offset // bn
    n_i = M // bm
    out_spec = pl.BlockSpec((bm, bn), lambda j, i: (i, j))
    in_specs = [
        pl.BlockSpec((bm, K), lambda j, i: (i, 0)),
        pl.BlockSpec((None, K, bn), lambda j, i: (layer, 0, j + off)),
    ]
    args = [x, w]
    if rope is not None:
        tiles_per_part, seq_tiles = (N // 2) // bn, seq // bm
        tab_spec = pl.BlockSpec((1, bm, LANES), lambda j, i: (j // tiles_per_part, i % seq_tiles, 0))
        in_specs += [tab_spec, tab_spec]
        args += [cos_tab, sin_tab]
    elif col_scale is not None:
        in_specs.append(pl.BlockSpec((1, bn), lambda j, i: (0, j)))
        args.append(col_scale.reshape(1, N).astype(F32))
    if residual is not None:
        in_specs.append(out_spec)
        args.append(residual)
    out_specs, out_shape = out_spec, jax.ShapeDtypeStruct((M, N), out_dtype)
    if narrow is not None:
        src, src_layer = narrow
        rows, width = src.shape[1], src.shape[2]
        slab = rows // ((N // bn) * n_i)
        assert slab * (N // bn) * n_i == rows and slab % HALO_ROWS == 0
        in_specs.append(pl.BlockSpec((None, slab, width), lambda j, i: (src_layer, j * n_i + i, 0)))
        args.append(src)
        out_specs = [out_spec, pl.BlockSpec((slab, width), lambda j, i: (j * n_i + i, 0))]
        out_shape = [out_shape, jax.ShapeDtypeStruct((rows, width), BF16)]
    return pl.pallas_call(
        functools.partial(
            _mm_body, act=act, has_scale=col_scale is not None, rope=rope is not None, alpha=alpha,
            side=narrow is not None,
        ),
        grid=(N // bn, n_i),
        in_specs=in_specs,
        out_specs=out_specs,
        out_shape=out_shape,
        scratch_shapes=[pltpu.VMEM((K, bn), BF16)],
        compiler_params=_params("arbitrary", "arbitrary"),
        name="matmul",
    )(*args)


def _layer_norm_rows(v, g, b):
    mu = jnp.mean(v, axis=-1, keepdims=True)
    xc = v - mu
    var = jnp.mean(xc * xc, axis=-1, keepdims=True)
    return xc * lax.rsqrt(var + LN_EPS) * g + b


def _ln_body(v_ref, g_ref, b_ref, of_ref, ob_ref):
    out = _layer_norm_rows(v_ref[...], g_ref[...], b_ref[...])
    of_ref[...] = out
    ob_ref[...] = out.astype(BF16)


def _layer_norm(v, g, b, *, tm=256):
    M, D = v.shape
    tm = _tile(M, tm)
    row = pl.BlockSpec((tm, D), lambda i: (i, 0))
    vec = pl.BlockSpec((1, D), lambda i: (0, 0))
    return pl.pallas_call(
        _ln_body,
        grid=(M // tm,),
        in_specs=[row, vec, vec],
        out_specs=[row, row],
        out_shape=[jax.ShapeDtypeStruct((M, D), F32), jax.ShapeDtypeStruct((M, D), BF16)],
        compiler_params=_params("parallel"),
        name="layer_norm",
    )(v, g.reshape(1, D), b.reshape(1, D))


def _dilated_fused_body(q_ref, k_ref, v_ref, o_ref, q32, k32, v32, qc, kc, vc, ob, lb, *, patterns, unroll):
    S = q_ref.shape[1]
    q32[...] = q_ref[0].astype(F32)
    k32[...] = k_ref[0].astype(F32)
    v32[...] = v_ref[0].astype(F32)
    nt = (((1,), (1,)), ((), ()))
    for g, (window, d) in enumerate(patterns):
        blk = window // d
        sc = S // d
        n_blocks = sc // blk
        per_iter = min(n_blocks, unroll)
        classes = min(d, max(1, unroll // n_blocks))
        pad = sc + blk
        for c in range(classes):
            kc[c * pad : c * pad + blk, :] = jnp.zeros((blk, LANES), BF16)
            vc[c * pad : c * pad + blk, :] = jnp.zeros((blk, LANES), BF16)
        qi = lax.broadcasted_iota(jnp.int32, (blk, 2 * blk), 0)
        kj = lax.broadcasted_iota(jnp.int32, (blk, 2 * blk), 1)

        def class_body(rr, carry, g=g, d=d, blk=blk, sc=sc, n_blocks=n_blocks, per_iter=per_iter, classes=classes,
                       pad=pad, qi=qi, kj=kj):
            for c in range(classes):
                if d == 1:
                    qc[0:sc, :] = q_ref[0]
                    kc[blk:pad, :] = k_ref[0]
                    vc[blk:pad, :] = v_ref[0]
                else:
                    rows = pl.ds(rr * classes + c, sc, stride=d)
                    qc[c * sc : (c + 1) * sc, :] = q32[rows, :].astype(BF16)
                    kc[c * pad + blk : (c + 1) * pad, :] = k32[rows, :].astype(BF16)
                    vc[c * pad + blk : (c + 1) * pad, :] = v32[rows, :].astype(BF16)

            def blocks(it, carry2):
                units = [
                    (c, pl.multiple_of((it * per_iter + u) * blk, blk)) for c in range(classes) for u in range(per_iter)
                ]
                scores = [
                    lax.dot_general(
                        qc[pl.ds(c * sc + n0, blk), :],
                        kc[pl.ds(c * pad + n0, 2 * blk), :],
                        nt,
                        preferred_element_type=F32,
                    )
                    for c, n0 in units
                ]
                probs, dens, lses = [], [], []
                for (c, n0), s in zip(units, scores):
                    visible = (kj >= jnp.maximum(qi, blk - n0)) & (kj <= qi + blk)
                    s = jnp.where(visible, s, MASK_VALUE)
                    m = jnp.max(s, axis=1, keepdims=True)
                    p = jnp.exp(s - m)
                    den = jnp.sum(p, axis=1, keepdims=True)
                    probs.append(p.astype(BF16))
                    dens.append(den)
                    lses.append(m + jnp.log(den))
                for (c, n0), p, den, lse in zip(units, probs, dens, lses):
                    o = jnp.dot(p, vc[pl.ds(c * pad + n0, 2 * blk), :], preferred_element_type=F32) / den
                    out_rows = pl.ds(rr * classes + c + n0 * d, blk, stride=d) if d > 1 else pl.ds(n0, blk)
                    ob[g, out_rows, :] = o
                    lb[g, out_rows, :] = jnp.broadcast_to(lse, (blk, LANES))
                return carry2

            lax.fori_loop(0, n_blocks // per_iter, blocks, 0)
            return carry

        lax.fori_loop(0, d // classes, class_body, 0)

    merge_rows = min(S, 512)

    def merge(t, carry):
        rows = pl.ds(pl.multiple_of(t * merge_rows, merge_rows), merge_rows)
        lses = [lb[g, rows, :] for g in range(len(patterns))]
        m = functools.reduce(jnp.maximum, lses)
        es = [jnp.exp(l - m) for l in lses]
        tot = functools.reduce(lambda a, b: a + b, es)
        acc = functools.reduce(lambda a, b: a + b, [e * ob[g, rows, :] for g, e in enumerate(es)])
        o_ref[0, rows, :] = (acc / tot).astype(o_ref.dtype)
        return carry

    lax.fori_loop(0, S // merge_rows, merge, 0)


def _dilated_attention(qk, v, B, S, D, patterns, *, unroll=4):
    H = D // LANES
    for window, d in patterns:
        assert S % window == 0
    max_blk = max(w // d for w, d in patterns)
    head = lambda off: pl.BlockSpec((1, S, LANES), lambda b, h: (b, 0, off + h))
    return pl.pallas_call(
        functools.partial(_dilated_fused_body, patterns=patterns, unroll=unroll),
        grid=(B, H),
        in_specs=[head(0), head(H), head(0)],
        out_specs=head(0),
        out_shape=jax.ShapeDtypeStruct((B, S, D), BF16),
        scratch_shapes=[
            pltpu.VMEM((S, LANES), F32),
            pltpu.VMEM((S, LANES), F32),
            pltpu.VMEM((S, LANES), F32),
            pltpu.VMEM((S, LANES), BF16),
            pltpu.VMEM((S + unroll * max_blk, LANES), BF16),
            pltpu.VMEM((S + unroll * max_blk, LANES), BF16),
            pltpu.VMEM((len(patterns), S, LANES), F32),
            pltpu.VMEM((len(patterns), S, LANES), F32),
        ],
        compiler_params=_params("parallel", "parallel"),
        name="dilated_attention",
    )(qk.reshape(B, S, 2 * D), qk.reshape(B, S, 2 * D), v.reshape(B, S, D))


def _rope_tables(S, hd, q_scale):
    pos = jnp.arange(S, dtype=F32)
    inv = ROPE_THETA ** (-jnp.arange(0, hd, 2, dtype=F32) / hd)
    ang = pos[:, None] * inv[None, :]
    ang = jnp.concatenate([ang, ang], axis=-1)
    sign = jnp.concatenate([-jnp.ones((hd // 2,), F32), jnp.ones((hd // 2,), F32)])
    cos, sin = jnp.cos(ang), jnp.sin(ang) * sign
    return jnp.stack([cos * q_scale, cos]), jnp.stack([sin * q_scale, sin])


def _mixer_dilated(xf, xb, wqkv, wo, layer, alpha, B, S, D):
    hd = D // N_HEADS
    assert hd == LANES
    cos_tab, sin_tab = _rope_tables(S, hd, hd**-0.5)
    qk = _matmul(xb, wqkv, layer, n_cols=2 * D, out_dtype=BF16, rope=(cos_tab, sin_tab, S))
    v = _matmul(xb, wqkv, layer, n_cols=D, col_offset=2 * D, out_dtype=BF16)
    o = _dilated_attention(qk, v, B, S, D, DILATED_PATTERNS)
    return _matmul(o.reshape(B * S, D), wo, layer, n_cols=D, out_dtype=F32, residual=xf, alpha=alpha)


def _pool_ln_body(x_ref, xh_ref, w_ref, sc_ref, g_ref, b_ref, of_ref, ob_ref, xs_ref, y_ref, *, alpha, ts):
    i = pl.program_id(1)
    x = x_ref[0]
    xs_ref[0:HALO_ROWS, :] = jnp.where(i > 0, xh_ref[0], 0.0)
    xs_ref[HALO_ROWS:, :] = x
    cg = x.shape[1] // len(POOL_WINDOWS)
    t1 = (i * ts + 1 + lax.broadcasted_iota(jnp.int32, (ts, 1), 0)).astype(F32)
    for g, win in enumerate(POOL_WINDOWS):
        cs = slice(g * cg, (g + 1) * cg)
        tot = x[:, cs]
        for j in range(1, win):
            tot = tot + xs_ref[pl.ds(HALO_ROWS - j, ts), cs]
        pooled = tot / jnp.minimum(t1, float(win)) - x[:, cs]
        y = jnp.dot(pooled.astype(BF16), w_ref[g], preferred_element_type=F32)
        y_ref[:, cs] = y * sc_ref[:, cs]
    out = _layer_norm_rows(alpha * x + y_ref[...], g_ref[...], b_ref[...])
    of_ref[0] = out
    ob_ref[0] = out.astype(BF16)


def _mixer_pool_ln(xf, wgrp, scale, g, b, alpha, B, S, D, *, ts=256):
    ts = _tile(S, ts)
    assert ts % HALO_ROWS == 0 and max(POOL_WINDOWS) <= HALO_ROWS
    G, cg, _ = wgrp.shape
    x3 = xf.reshape(B, S, D)
    row = pl.BlockSpec((1, ts, D), lambda bb, i: (bb, i, 0))
    halo = pl.BlockSpec((1, HALO_ROWS, D), lambda bb, i: (bb, jnp.maximum(i * (ts // HALO_ROWS) - 1, 0), 0))
    vec = pl.BlockSpec((1, D), lambda bb, i: (0, 0))
    of, ob = pl.pallas_call(
        functools.partial(_pool_ln_body, alpha=alpha, ts=ts),
        grid=(B, S // ts),
        in_specs=[row, halo, pl.BlockSpec((G, cg, cg), lambda bb, i: (0, 0, 0)), vec, vec, vec],
        out_specs=[row, row],
        out_shape=[jax.ShapeDtypeStruct((B, S, D), F32), jax.ShapeDtypeStruct((B, S, D), BF16)],
        scratch_shapes=[pltpu.VMEM((ts + HALO_ROWS, D), F32), pltpu.VMEM((ts, D), F32)],
        compiler_params=_params("parallel", "parallel"),
        name="pool_ln",
    )(x3, x3, wgrp.astype(BF16), scale.reshape(1, D), g.reshape(1, D), b.reshape(1, D))
    return of.reshape(B * S, D), ob.reshape(B * S, D)


def _fgate_body(z_ref, bf_ref, c_ref):
    z = z_ref[0] + bf_ref[...]
    c = jnp.minimum(z, 0.0) - jnp.log1p(jnp.exp(-jnp.abs(z)))
    S = c.shape[0]
    row = lax.broadcasted_iota(jnp.int32, c.shape, 0)
    shift = 1
    while shift < S:
        c = c + jnp.where(row >= shift, pltpu.roll(c, shift, 0), 0.0)
        shift *= 2
    c_ref[0] = c


def _forget_cumsum(z, bias_row):
    B, S, W = z.shape
    blk = pl.BlockSpec((1, S, W), lambda b: (b, 0, 0))
    return pl.pallas_call(
        _fgate_body,
        grid=(B,),
        in_specs=[blk, pl.BlockSpec((1, W), lambda b: (0, 0))],
        out_specs=blk,
        out_shape=jax.ShapeDtypeStruct((B, S, W), F32),
        compiler_params=_params("parallel"),
        name="forget_cumsum",
    )(z, bias_row)


def _fox_body(q_ref, k_ref, v_ref, cc_ref, o_ref, qa_ref, ka_ref, *, bq, bk, heads):
    S = q_ref.shape[1]
    diag_steps = bq // bk
    lane = lax.broadcasted_iota(jnp.int32, (S, LANES), 1)
    for t in range(heads):
        head_lane = pl.program_id(1) * heads + t
        c = jnp.sum(jnp.where(lane == head_lane, cc_ref[0], 0.0), axis=1, keepdims=True) * LOG2_E
        hi = c.astype(BF16).astype(F32)
        mid = (c - hi).astype(BF16).astype(F32)
        lo = c - hi - mid

        def pieces(sign, first, hi=hi, mid=mid, lo=lo):
            return (
                jnp.where(lane == first, sign * hi, 0.0)
                + jnp.where(lane == first + 1, sign * mid, 0.0)
                + jnp.where(lane == first + 2, sign * lo, 0.0)
            )

        qa_ref[t] = (pieces(1.0, 0) + jnp.where((lane >= 3) & (lane < 6), 1.0, 0.0)).astype(BF16)
        ka_ref[t] = (jnp.where(lane < 3, 1.0, 0.0) + pieces(-1.0, 3)).astype(BF16)

    key_minus_query = lax.broadcasted_iota(jnp.int32, (bq, bk), 1) - lax.broadcasted_iota(jnp.int32, (bq, bk), 0)
    nt = (((1,), (1,)), ((), ()))
    across = lambda t: jnp.concatenate([t] * (bk // LANES), axis=1)
    cols = [slice(t * LANES, (t + 1) * LANES) for t in range(heads)]

    def q_block(i, carry):
        r0 = pl.multiple_of(i * bq, bq)
        rows = pl.ds(r0, bq)
        qs = [jnp.concatenate([q_ref[0, rows, cs], qa_ref[t, rows, :]], axis=1) for t, cs in enumerate(cols)]

        def k_step(c0, states, diag_offset):
            keys = pl.ds(c0, bk)
            scores = [
                lax.dot_general(
                    q, jnp.concatenate([k_ref[0, keys, cs], ka_ref[t, keys, :]], axis=1), nt, preferred_element_type=F32
                )
                for t, (q, cs) in enumerate(zip(qs, cols))
            ]
            new_states = []
            for s, (m, l, acc), cs in zip(scores, states, cols):
                if diag_offset is not None:
                    s = jnp.where(key_minus_query <= -diag_offset, s, MASK_VALUE)
                m_new = jnp.maximum(m, jnp.max(s, axis=1, keepdims=True))
                a = jnp.exp2(m - m_new)
                p = jnp.exp2(s - across(m_new))
                l = a * l + jnp.sum(p, axis=1, keepdims=True)
                acc = a * acc + jnp.dot(p.astype(BF16), v_ref[0, keys, cs], preferred_element_type=F32)
                new_states.append((m_new, l, acc))
            return tuple(new_states)

        init = (jnp.full((bq, LANES), MASK_VALUE, F32), jnp.zeros((bq, LANES), F32), jnp.zeros((bq, LANES), F32))
        states = lax.fori_loop(
            0, i * diag_steps, lambda j, st: k_step(pl.multiple_of(j * bk, bk), st, None), (init,) * heads
        )
        for jj in range(diag_steps):
            states = k_step(pl.multiple_of(r0 + jj * bk, bk), states, jj * bk)
        for (_, l, acc), cs in zip(states, cols):
            o_ref[0, rows, cs] = (acc / l).astype(o_ref.dtype)
        return carry

    lax.fori_loop(0, S // bq, q_block, 0)


def _fox_attention(proj, c_col, B, S, D, *, bq=1024, bk=1024, heads=2):
    bq = _tile(S, bq)
    bk = _tile(bq, bk)
    W = heads * LANES
    groups = D // W
    part = lambda p: pl.BlockSpec((1, S, W), lambda b, g: (b, 0, p * groups + g))
    return pl.pallas_call(
        functools.partial(_fox_body, bq=bq, bk=bk, heads=heads),
        grid=(B, groups),
        in_specs=[part(0), part(1), part(2), pl.BlockSpec((1, S, LANES), lambda b, g: (b, 0, 0))],
        out_specs=pl.BlockSpec((1, S, W), lambda b, g: (b, 0, g)),
        out_shape=jax.ShapeDtypeStruct((B, S, D), BF16),
        scratch_shapes=[pltpu.VMEM((heads, S, LANES), BF16), pltpu.VMEM((heads, S, LANES), BF16)],
        compiler_params=_params("parallel", "arbitrary"),
        name="fox_attention",
    )(proj, proj, proj, c_col)


def _mixer_fox(xf, xb, win, bf, wo, layer, alpha, B, S, D):
    H = N_HEADS
    assert D // H == LANES and H <= LANES
    col_scale = jnp.concatenate([jnp.full((D,), LANES**-0.5 * LOG2_E, F32), jnp.ones((2 * D,), F32)])
    proj = _matmul(xb, win, layer, n_cols=3 * D, out_dtype=BF16, col_scale=col_scale)
    w_gate = jnp.pad(win[layer, :, 3 * D :], ((0, 0), (0, LANES - H)))[None]
    z = _matmul(xb, w_gate, 0, n_cols=LANES, out_dtype=F32)
    bias_row = jnp.pad(bf[layer].astype(F32), (0, LANES - H)).reshape(1, LANES)
    c_col = _forget_cumsum(z.reshape(B, S, LANES), bias_row)
    o = _fox_attention(proj.reshape(B, S, 3 * D), c_col, B, S, D)
    return _matmul(o.reshape(B * S, D), wo, layer, n_cols=D, out_dtype=F32, residual=xf, alpha=alpha)


def _convgate_body(b_ref, c_ref, h_ref, ch_ref, hh_ref, w_ref, o_ref, zs_ref, *, ts):
    i = pl.program_id(1)
    z = c_ref[0].astype(F32) * h_ref[0].astype(F32)
    zh = ch_ref[0].astype(F32) * hh_ref[0].astype(F32)
    zs_ref[0:HALO_ROWS, :] = jnp.where(i > 0, zh, 0.0)
    zs_ref[HALO_ROWS:, :] = z
    w = w_ref[...]
    conv = w[CONV_WIDTH - 1 : CONV_WIDTH] * z
    for j in range(1, CONV_WIDTH):
        tap = CONV_WIDTH - 1 - j
        conv = conv + w[tap : tap + 1] * zs_ref[pl.ds(HALO_ROWS - j, ts), :]
    o_ref[0] = (b_ref[0].astype(F32) * conv).astype(o_ref.dtype)


def _conv_gate(proj, wconv, B, S, D, *, ts=512, tc=1024):
    ts, tc = _tile(S, ts), _tile(D, tc)
    assert ts % HALO_ROWS == 0 and CONV_WIDTH - 1 <= HALO_ROWS
    nct = D // tc
    cur = lambda part: pl.BlockSpec((1, ts, tc), lambda b, i, j: (b, i, part * nct + j))
    halo = lambda part: pl.BlockSpec(
        (1, HALO_ROWS, tc), lambda b, i, j: (b, jnp.maximum(i * (ts // HALO_ROWS) - 1, 0), part * nct + j)
    )
    return pl.pallas_call(
        functools.partial(_convgate_body, ts=ts),
        grid=(B, S // ts, nct),
        in_specs=[cur(0), cur(1), cur(2), halo(1), halo(2), pl.BlockSpec((CONV_WIDTH, tc), lambda b, i, j: (0, j))],
        out_specs=pl.BlockSpec((1, ts, tc), lambda b, i, j: (b, i, j)),
        out_shape=jax.ShapeDtypeStruct((B, S, D), BF16),
        scratch_shapes=[pltpu.VMEM((ts + HALO_ROWS, tc), F32)],
        compiler_params=_params("parallel", "parallel", "parallel"),
        name="conv_gate",
    )(proj, proj, proj, proj, proj, wconv.astype(F32))


def _mixer_conv(xf, xb, win, wconv, wout, layer, alpha, B, S, D):
    proj = _matmul(xb, win, layer, n_cols=3 * D, out_dtype=BF16)
    u = _conv_gate(proj.reshape(B, S, 3 * D), wconv[layer], B, S, D)
    return _matmul(u.reshape(B * S, D), wout, layer, n_cols=D, out_dtype=F32, residual=xf, alpha=alpha)


def kernel(x, ln_g, ln_b, mlp_w1, mlp_w2, a_wqkv, a_wo, b_wgrp, b_scale, c_win, c_bf, c_wo, d_win, d_conv, d_wout):
    B, S, D = x.shape
    depth = ln_g.shape[0]
    n_mixers = 4
    alpha = (2.0 * depth) ** 0.25
    xf = x.reshape(B * S, D).astype(F32)
    xb = xf.astype(BF16)
    for i in range(depth):
        kind, j = i % n_mixers, i // n_mixers
        if kind == 1:
            xf, xb = _mixer_pool_ln(xf, b_wgrp[j], b_scale[j], ln_g[i, 0], ln_b[i, 0], alpha, B, S, D)
        else:
            if kind == 0:
                v = _mixer_dilated(xf, xb, a_wqkv, a_wo, j, alpha, B, S, D)
            elif kind == 2:
                v = _mixer_fox(xf, xb, c_win, c_bf, c_wo, j, alpha, B, S, D)
            else:
                v = _mixer_conv(xf, xb, d_win, d_conv, d_wout, j, alpha, B, S, D)
            xf, xb = _layer_norm(v, ln_g[i, 0], ln_b[i, 0])
        hid, w2b = _matmul(xb, mlp_w1, i, n_cols=mlp_w1.shape[2], out_dtype=BF16, act="relu2", narrow=(mlp_w2, i))
        v = _matmul_bf16w(hid, w2b[None], 0, xf, alpha)
        xf, xb = _layer_norm(v, ln_g[i, 1], ln_b[i, 1])
    return xf.reshape(B, S, D).astype(x.dtype)
```

```python
import functools

import jax
import jax.numpy as jnp
from jax import lax
from jax.experimental import pallas as pl
from jax.experimental.pallas import tpu as pltpu

N_HEADS = 32
ROPE_THETA = 10000.0
DILATED_PATTERNS = ((128, 1), (512, 4), (2048, 16))
POOL_WINDOWS = (2, 4, 8, 16)
CONV_WIDTH = 3
LN_EPS = 1e-5
LANES = 128
HALO_ROWS = 16
VMEM_LIMIT_BYTES = 56 * 1024 * 1024
VMEM_LIMIT_LARGE_BYTES = 60000 * 1024
MASK_VALUE = -1e30
LOG2_E = 1.4426950408889634

BF16 = jnp.bfloat16
F32 = jnp.float32


def _tile(n, pref):
    t = min(n, pref)
    while n % t:
        assert t % 2 == 0, (n, pref)
        t //= 2
    return t


def _params(*sem, vmem_limit_bytes=VMEM_LIMIT_BYTES):
    return pltpu.CompilerParams(dimension_semantics=sem, vmem_limit_bytes=vmem_limit_bytes)


def _mm_acc_body(x_ref, w_ref, r_ref, o_ref, *, alpha):
    acc = jnp.dot(x_ref[...], w_ref[...], preferred_element_type=F32)
    k = pl.program_id(2)

    @pl.when(k == 0)
    def _():
        o_ref[...] = alpha * r_ref[...] + acc

    @pl.when(k > 0)
    def _():
        o_ref[...] += acc


def _matmul_bf16w(x, w, layer, residual, alpha, *, bm=1024, bn=1024, bk=4096):
    M, K = x.shape
    N = w.shape[2]
    bm, bn, bk = _tile(M, bm), _tile(N, bn), _tile(K, bk)
    out_spec = pl.BlockSpec((bm, bn), lambda i, j, k: (i, j))
    return pl.pallas_call(
        functools.partial(_mm_acc_body, alpha=alpha),
        grid=(M // bm, N // bn, K // bk),
        in_specs=[
            pl.BlockSpec((bm, bk), lambda i, j, k: (i, k)),
            pl.BlockSpec((None, bk, bn), lambda i, j, k: (layer, k, j)),
            out_spec,
        ],
        out_specs=out_spec,
        out_shape=jax.ShapeDtypeStruct((M, N), F32),
        compiler_params=_params("parallel", "parallel", "arbitrary", vmem_limit_bytes=VMEM_LIMIT_LARGE_BYTES),
        name="matmul_ktiled",
    )(x, w, residual)


def _mm_body(*refs, act, has_scale, rope, alpha, n_side, narrow_w):
    x_ref, w_ref = refs[0], refs[1]
    n_extra = 2 if rope else (1 if has_scale else 0)
    n_in = 2 + n_extra + (alpha is not None) + n_side
    o_ref = refs[n_in]
    for t in range(n_side):
        refs[n_in + 1 + t][...] = refs[n_in - n_side + t][...].astype(BF16)

    if narrow_w:
        wb_ref = refs[-1]

        @pl.when(pl.program_id(1) == 0)
        def _():
            wb_ref[...] = w_ref[...].astype(BF16)

        w_ref = wb_ref

    acc = jnp.dot(x_ref[...], w_ref[...], preferred_element_type=F32)
    if rope:
        cos, sin = refs[2][0], refs[3][0]
        for h in range(acc.shape[1] // LANES):
            cs = slice(h * LANES, (h + 1) * LANES)
            a = acc[:, cs]
            o_ref[:, cs] = (a * cos + pltpu.roll(a, LANES // 2, 1) * sin).astype(o_ref.dtype)
        return
    if act == "relu2":
        acc = jnp.maximum(acc, 0.0)
        acc = acc * acc
    if has_scale:
        acc = acc * refs[2][...]
    if alpha is not None:
        acc = alpha * refs[2 + n_extra][...] + acc
    o_ref[...] = acc.astype(o_ref.dtype)


def _matmul(x, w, layer, *, n_cols, out_dtype, col_offset=0, act=None, col_scale=None, rope=None, residual=None,
            alpha=None, narrow=(), bm=1024, bn=None):
    M, K = x.shape
    N = n_cols
    narrow_w = w.dtype != BF16
    if bn is None:
        bn = 512 if narrow_w else 1024
    if rope is not None:
        cos_tab, sin_tab, seq = rope
        bm, bn = _tile(seq, bm), _tile(N // 2, bn)
    else:
        bm, bn = _tile(M, bm), _tile(N, bn)
    assert col_offset % bn == 0 and (residual is None) == (alpha is None)
    off = col_offset // bn
    n_i = M // bm
    n_steps = (N // bn) * n_i
    out_spec = pl.BlockSpec((bm, bn), lambda j, i: (i, j))
    in_specs = [
        pl.BlockSpec((bm, K), lambda j, i: (i, 0)),
        pl.BlockSpec((None, K, bn), lambda j, i: (layer, 0, j + off)),
    ]
    args = [x, w]
    if rope is not None:
        tiles_per_part, seq_tiles = (N // 2) // bn, seq // bm
        tab_spec = pl.BlockSpec((1, bm, LANES), lambda j, i: (j // tiles_per_part, i % seq_tiles, 0))
        in_specs += [tab_spec, tab_spec]
        args += [cos_tab, sin_tab]
    elif col_scale is not None:
        in_specs.append(pl.BlockSpec((1, bn), lambda j, i: (0, j)))
        args.append(col_scale.reshape(1, N).astype(F32))
    if residual is not None:
        in_specs.append(out_spec)
        args.append(residual)
    out_specs, out_shape = [out_spec], [jax.ShapeDtypeStruct((M, N), out_dtype)]
    for src, src_layer in narrow:
        rows, width = src.shape[1], src.shape[2]
        slab = rows // n_steps
        assert slab * n_steps == rows and slab % HALO_ROWS == 0
        in_specs.append(pl.BlockSpec((None, slab, width), lambda j, i, src_layer=src_layer: (src_layer, j * n_i + i, 0)))
        args.append(src)
        out_specs.append(pl.BlockSpec((slab, width), lambda j, i: (j * n_i + i, 0)))
        out_shape.append(jax.ShapeDtypeStruct((rows, width), BF16))
    outs = pl.pallas_call(
        functools.partial(
            _mm_body, act=act, has_scale=col_scale is not None, rope=rope is not None, alpha=alpha,
            n_side=len(narrow), narrow_w=narrow_w,
        ),
        grid=(N // bn, n_i),
        in_specs=in_specs,
        out_specs=out_specs,
        out_shape=out_shape,
        scratch_shapes=[pltpu.VMEM((K, bn), BF16)] if narrow_w else [],
        compiler_params=_params("arbitrary", "arbitrary"),
        name="matmul",
    )(*args)
    return (outs[0], outs[1:]) if narrow else outs[0]


def _layer_norm_rows(v, g, b):
    mu = jnp.mean(v, axis=-1, keepdims=True)
    xc = v - mu
    var = jnp.mean(xc * xc, axis=-1, keepdims=True)
    return xc * lax.rsqrt(var + LN_EPS) * g + b


def _ln_body(v_ref, g_ref, b_ref, of_ref, ob_ref):
    out = _layer_norm_rows(v_ref[...], g_ref[...], b_ref[...])
    of_ref[...] = out
    ob_ref[...] = out.astype(BF16)


def _layer_norm(v, g, b, *, tm=256):
    M, D = v.shape
    tm = _tile(M, tm)
    row = pl.BlockSpec((tm, D), lambda i: (i, 0))
    vec = pl.BlockSpec((1, D), lambda i: (0, 0))
    return pl.pallas_call(
        _ln_body,
        grid=(M // tm,),
        in_specs=[row, vec, vec],
        out_specs=[row, row],
        out_shape=[jax.ShapeDtypeStruct((M, D), F32), jax.ShapeDtypeStruct((M, D), BF16)],
        compiler_params=_params("parallel"),
        name="layer_norm",
    )(v, g.reshape(1, D), b.reshape(1, D))


def _dilated_fused_body(q_ref, k_ref, v_ref, o_ref, q32, k32, v32, qc, kc, vc, ob, lb, *, patterns, unroll):
    S = q_ref.shape[1]
    q32[...] = q_ref[0].astype(F32)
    k32[...] = k_ref[0].astype(F32)
    v32[...] = v_ref[0].astype(F32)
    nt = (((1,), (1,)), ((), ()))
    for g, (window, d) in enumerate(patterns):
        blk = window // d
        sc = S // d
        n_blocks = sc // blk
        per_iter = min(n_blocks, unroll)
        classes = min(d, max(1, unroll // n_blocks))
        pad = sc + blk
        for c in range(classes):
            kc[c * pad : c * pad + blk, :] = jnp.zeros((blk, LANES), BF16)
            vc[c * pad : c * pad + blk, :] = jnp.zeros((blk, LANES), BF16)
        qi = lax.broadcasted_iota(jnp.int32, (blk, 2 * blk), 0)
        kj = lax.broadcasted_iota(jnp.int32, (blk, 2 * blk), 1)

        def class_body(rr, carry, g=g, d=d, blk=blk, sc=sc, n_blocks=n_blocks, per_iter=per_iter, classes=classes,
                       pad=pad, qi=qi, kj=kj):
            for c in range(classes):
                if d == 1:
                    qc[0:sc, :] = q_ref[0]
                    kc[blk:pad, :] = k_ref[0]
                    vc[blk:pad, :] = v_ref[0]
                else:
                    rows = pl.ds(rr * classes + c, sc, stride=d)
                    qc[c * sc : (c + 1) * sc, :] = q32[rows, :].astype(BF16)
                    kc[c * pad + blk : (c + 1) * pad, :] = k32[rows, :].astype(BF16)
                    vc[c * pad + blk : (c + 1) * pad, :] = v32[rows, :].astype(BF16)

            def blocks(it, carry2):
                units = [
                    (c, pl.multiple_of((it * per_iter + u) * blk, blk)) for c in range(classes) for u in range(per_iter)
                ]
                scores = [
                    lax.dot_general(
                        qc[pl.ds(c * sc + n0, blk), :],
                        kc[pl.ds(c * pad + n0, 2 * blk), :],
                        nt,
                        preferred_element_type=F32,
                    )
                    for c, n0 in units
                ]
                probs, dens, lses = [], [], []
                for (c, n0), s in zip(units, scores):
                    visible = (kj >= jnp.maximum(qi, blk - n0)) & (kj <= qi + blk)
                    s = jnp.where(visible, s, MASK_VALUE)
                    m = jnp.max(s, axis=1, keepdims=True)
                    p = jnp.exp(s - m)
                    den = jnp.sum(p, axis=1, keepdims=True)
                    probs.append(p.astype(BF16))
                    dens.append(den)
                    lses.append(m + jnp.log(den))
                for (c, n0), p, den, lse in zip(units, probs, dens, lses):
                    o = jnp.dot(p, vc[pl.ds(c * pad + n0, 2 * blk), :], preferred_element_type=F32) / den
                    out_rows = pl.ds(rr * classes + c + n0 * d, blk, stride=d) if d > 1 else pl.ds(n0, blk)
                    ob[g, out_rows, :] = o
                    lb[g, out_rows, :] = jnp.broadcast_to(lse, (blk, LANES))
                return carry2

            lax.fori_loop(0, n_blocks // per_iter, blocks, 0)
            return carry

        lax.fori_loop(0, d // classes, class_body, 0)

    merge_rows = min(S, 512)

    def merge(t, carry):
        rows = pl.ds(pl.multiple_of(t * merge_rows, merge_rows), merge_rows)
        lses = [lb[g, rows, :] for g in range(len(patterns))]
        m = functools.reduce(jnp.maximum, lses)
        es = [jnp.exp(l - m) for l in lses]
        tot = functools.reduce(lambda a, b: a + b, es)
        acc = functools.reduce(lambda a, b: a + b, [e * ob[g, rows, :] for g, e in enumerate(es)])
        o_ref[0, rows, :] = (acc / tot).astype(o_ref.dtype)
        return carry

    lax.fori_loop(0, S // merge_rows, merge, 0)


def _dilated_attention(qk, v, B, S, D, patterns, *, unroll=4):
    H = D // LANES
    for window, d in patterns:
        assert S % window == 0
    max_blk = max(w // d for w, d in patterns)
    head = lambda off: pl.BlockSpec((1, S, LANES), lambda b, h: (b, 0, off + h))
    return pl.pallas_call(
        functools.partial(_dilated_fused_body, patterns=patterns, unroll=unroll),
        grid=(B, H),
        in_specs=[head(0), head(H), head(0)],
        out_specs=head(0),
        out_shape=jax.ShapeDtypeStruct((B, S, D), BF16),
        scratch_shapes=[
            pltpu.VMEM((S, LANES), F32),
            pltpu.VMEM((S, LANES), F32),
            pltpu.VMEM((S, LANES), F32),
            pltpu.VMEM((S, LANES), BF16),
            pltpu.VMEM((S + unroll * max_blk, LANES), BF16),
            pltpu.VMEM((S + unroll * max_blk, LANES), BF16),
            pltpu.VMEM((len(patterns), S, LANES), F32),
            pltpu.VMEM((len(patterns), S, LANES), F32),
        ],
        compiler_params=_params("parallel", "parallel"),
        name="dilated_attention",
    )(qk.reshape(B, S, 2 * D), qk.reshape(B, S, 2 * D), v.reshape(B, S, D))


def _rope_tables(S, hd, q_scale):
    pos = jnp.arange(S, dtype=F32)
    inv = ROPE_THETA ** (-jnp.arange(0, hd, 2, dtype=F32) / hd)
    ang = pos[:, None] * inv[None, :]
    ang = jnp.concatenate([ang, ang], axis=-1)
    sign = jnp.concatenate([-jnp.ones((hd // 2,), F32), jnp.ones((hd // 2,), F32)])
    cos, sin = jnp.cos(ang), jnp.sin(ang) * sign
    return jnp.stack([cos * q_scale, cos]), jnp.stack([sin * q_scale, sin])


def _mixer_dilated(xf, xb, wqkv, wo, layer, alpha, B, S, D, narrow=()):
    hd = D // N_HEADS
    assert hd == LANES
    cos_tab, sin_tab = _rope_tables(S, hd, hd**-0.5)
    qk = _matmul(xb, wqkv, layer, n_cols=2 * D, out_dtype=BF16, rope=(cos_tab, sin_tab, S), narrow=narrow)
    if narrow:
        qk, narrowed = qk
    v = _matmul(xb, wqkv, layer, n_cols=D, col_offset=2 * D, out_dtype=BF16)
    o = _dilated_attention(qk, v, B, S, D, DILATED_PATTERNS)
    out = _matmul(o.reshape(B * S, D), wo, layer, n_cols=D, out_dtype=F32, residual=xf, alpha=alpha)
    return (out, narrowed) if narrow else out


def _pool_ln_body(x_ref, xh_ref, w_ref, sc_ref, g_ref, b_ref, of_ref, ob_ref, xs_ref, y_ref, *, alpha, ts):
    i = pl.program_id(1)
    x = x_ref[0]
    xs_ref[0:HALO_ROWS, :] = jnp.where(i > 0, xh_ref[0], 0.0)
    xs_ref[HALO_ROWS:, :] = x
    cg = x.shape[1] // len(POOL_WINDOWS)
    t1 = (i * ts + 1 + lax.broadcasted_iota(jnp.int32, (ts, 1), 0)).astype(F32)
    for g, win in enumerate(POOL_WINDOWS):
        cs = slice(g * cg, (g + 1) * cg)
        tot = x[:, cs]
        for j in range(1, win):
            tot = tot + xs_ref[pl.ds(HALO_ROWS - j, ts), cs]
        pooled = tot / jnp.minimum(t1, float(win)) - x[:, cs]
        y = jnp.dot(pooled.astype(BF16), w_ref[g], preferred_element_type=F32)
        y_ref[:, cs] = y * sc_ref[:, cs]
    out = _layer_norm_rows(alpha * x + y_ref[...], g_ref[...], b_ref[...])
    of_ref[0] = out
    ob_ref[0] = out.astype(BF16)


def _mixer_pool_ln(xf, wgrp, scale, g, b, alpha, B, S, D, *, ts=256):
    ts = _tile(S, ts)
    assert ts % HALO_ROWS == 0 and max(POOL_WINDOWS) <= HALO_ROWS
    G, cg, _ = wgrp.shape
    x3 = xf.reshape(B, S, D)
    row = pl.BlockSpec((1, ts, D), lambda bb, i: (bb, i, 0))
    halo = pl.BlockSpec((1, HALO_ROWS, D), lambda bb, i: (bb, jnp.maximum(i * (ts // HALO_ROWS) - 1, 0), 0))
    vec = pl.BlockSpec((1, D), lambda bb, i: (0, 0))
    of, ob = pl.pallas_call(
        functools.partial(_pool_ln_body, alpha=alpha, ts=ts),
        grid=(B, S // ts),
        in_specs=[row, halo, pl.BlockSpec((G, cg, cg), lambda bb, i: (0, 0, 0)), vec, vec, vec],
        out_specs=[row, row],
        out_shape=[jax.ShapeDtypeStruct((B, S, D), F32), jax.ShapeDtypeStruct((B, S, D), BF16)],
        scratch_shapes=[pltpu.VMEM((ts + HALO_ROWS, D), F32), pltpu.VMEM((ts, D), F32)],
        compiler_params=_params("parallel", "parallel"),
        name="pool_ln",
    )(x3, x3, wgrp.astype(BF16), scale.reshape(1, D), g.reshape(1, D), b.reshape(1, D))
    return of.reshape(B * S, D), ob.reshape(B * S, D)


def _fgate_body(z_ref, bf_ref, c_ref):
    z = z_ref[0] + bf_ref[...]
    c = jnp.minimum(z, 0.0) - jnp.log1p(jnp.exp(-jnp.abs(z)))
    S = c.shape[0]
    row = lax.broadcasted_iota(jnp.int32, c.shape, 0)
    shift = 1
    while shift < S:
        c = c + jnp.where(row >= shift, pltpu.roll(c, shift, 0), 0.0)
        shift *= 2
    c_ref[0] = c


def _forget_cumsum(z, bias_row):
    B, S, W = z.shape
    blk = pl.BlockSpec((1, S, W), lambda b: (b, 0, 0))
    return pl.pallas_call(
        _fgate_body,
        grid=(B,),
        in_specs=[blk, pl.BlockSpec((1, W), lambda b: (0, 0))],
        out_specs=blk,
        out_shape=jax.ShapeDtypeStruct((B, S, W), F32),
        compiler_params=_params("parallel"),
        name="forget_cumsum",
    )(z, bias_row)


def _fox_body(q_ref, k_ref, v_ref, cc_ref, o_ref, qa_ref, ka_ref, *, bq, bk, heads):
    S = q_ref.shape[1]
    diag_steps = bq // bk
    lane = lax.broadcasted_iota(jnp.int32, (S, LANES), 1)
    for t in range(heads):
        head_lane = pl.program_id(1) * heads + t
        c = jnp.sum(jnp.where(lane == head_lane, cc_ref[0], 0.0), axis=1, keepdims=True) * LOG2_E
        hi = c.astype(BF16).astype(F32)
        mid = (c - hi).astype(BF16).astype(F32)
        lo = c - hi - mid

        def pieces(sign, first, hi=hi, mid=mid, lo=lo):
            return (
                jnp.where(lane == first, sign * hi, 0.0)
                + jnp.where(lane == first + 1, sign * mid, 0.0)
                + jnp.where(lane == first + 2, sign * lo, 0.0)
            )

        qa_ref[t] = (pieces(1.0, 0) + jnp.where((lane >= 3) & (lane < 6), 1.0, 0.0)).astype(BF16)
        ka_ref[t] = (jnp.where(lane < 3, 1.0, 0.0) + pieces(-1.0, 3)).astype(BF16)

    key_minus_query = lax.broadcasted_iota(jnp.int32, (bq, bk), 1) - lax.broadcasted_iota(jnp.int32, (bq, bk), 0)
    nt = (((1,), (1,)), ((), ()))
    across = lambda t: jnp.concatenate([t] * (bk // LANES), axis=1)
    cols = [slice(t * LANES, (t + 1) * LANES) for t in range(heads)]

    def q_block(i, carry):
        r0 = pl.multiple_of(i * bq, bq)
        rows = pl.ds(r0, bq)
        qs = [jnp.concatenate([q_ref[0, rows, cs], qa_ref[t, rows, :]], axis=1) for t, cs in enumerate(cols)]

        def k_step(c0, states, diag_offset):
            keys = pl.ds(c0, bk)
            scores = [
                lax.dot_general(
                    q, jnp.concatenate([k_ref[0, keys, cs], ka_ref[t, keys, :]], axis=1), nt, preferred_element_type=F32
                )
                for t, (q, cs) in enumerate(zip(qs, cols))
            ]
            new_states = []
            for s, (m, l, acc), cs in zip(scores, states, cols):
                if diag_offset is not None:
                    s = jnp.where(key_minus_query <= -diag_offset, s, MASK_VALUE)
                m_new = jnp.maximum(m, jnp.max(s, axis=1, keepdims=True))
                a = jnp.exp2(m - m_new)
                p = jnp.exp2(s - across(m_new))
                l = a * l + jnp.sum(p, axis=1, keepdims=True)
                acc = a * acc + jnp.dot(p.astype(BF16), v_ref[0, keys, cs], preferred_element_type=F32)
                new_states.append((m_new, l, acc))
            return tuple(new_states)

        init = (jnp.full((bq, LANES), MASK_VALUE, F32), jnp.zeros((bq, LANES), F32), jnp.zeros((bq, LANES), F32))
        states = lax.fori_loop(
            0, i * diag_steps, lambda j, st: k_step(pl.multiple_of(j * bk, bk), st, None), (init,) * heads
        )
        for jj in range(diag_steps):
            states = k_step(pl.multiple_of(r0 + jj * bk, bk), states, jj * bk)
        for (_, l, acc), cs in zip(states, cols):
            o_ref[0, rows, cs] = (acc / l).astype(o_ref.dtype)
        return carry

    lax.fori_loop(0, S // bq, q_block, 0)


def _fox_attention(proj, c_col, B, S, D, *, bq=1024, bk=1024, heads=2):
    bq = _tile(S, bq)
    bk = _tile(bq, bk)
    W = heads * LANES
    groups = D // W
    part = lambda p: pl.BlockSpec((1, S, W), lambda b, g: (b, 0, p * groups + g))
    return pl.pallas_call(
        functools.partial(_fox_body, bq=bq, bk=bk, heads=heads),
        grid=(B, groups),
        in_specs=[part(0), part(1), part(2), pl.BlockSpec((1, S, LANES), lambda b, g: (b, 0, 0))],
        out_specs=pl.BlockSpec((1, S, W), lambda b, g: (b, 0, g)),
        out_shape=jax.ShapeDtypeStruct((B, S, D), BF16),
        scratch_shapes=[pltpu.VMEM((heads, S, LANES), BF16), pltpu.VMEM((heads, S, LANES), BF16)],
        compiler_params=_params("parallel", "arbitrary"),
        name="fox_attention",
    )(proj, proj, proj, c_col)


def _mixer_fox(xf, xb, win, bf, wo, layer, alpha, B, S, D):
    H = N_HEADS
    assert D // H == LANES and H <= LANES
    col_scale = jnp.concatenate([jnp.full((D,), LANES**-0.5 * LOG2_E, F32), jnp.ones((2 * D,), F32)])
    proj = _matmul(xb, win, layer, n_cols=3 * D, out_dtype=BF16, col_scale=col_scale)
    w_gate = jnp.pad(win[layer, :, 3 * D :], ((0, 0), (0, LANES - H)))[None]
    z = _matmul(xb, w_gate, 0, n_cols=LANES, out_dtype=F32)
    bias_row = jnp.pad(bf[layer].astype(F32), (0, LANES - H)).reshape(1, LANES)
    c_col = _forget_cumsum(z.reshape(B, S, LANES), bias_row)
    o = _fox_attention(proj.reshape(B, S, 3 * D), c_col, B, S, D)
    return _matmul(o.reshape(B * S, D), wo, layer, n_cols=D, out_dtype=F32, residual=xf, alpha=alpha)


def _convgate_body(b_ref, c_ref, h_ref, ch_ref, hh_ref, w_ref, o_ref, zs_ref, *, ts):
    i = pl.program_id(1)
    z = c_ref[0].astype(F32) * h_ref[0].astype(F32)
    zh = ch_ref[0].astype(F32) * hh_ref[0].astype(F32)
    zs_ref[0:HALO_ROWS, :] = jnp.where(i > 0, zh, 0.0)
    zs_ref[HALO_ROWS:, :] = z
    w = w_ref[...]
    conv = w[CONV_WIDTH - 1 : CONV_WIDTH] * z
    for j in range(1, CONV_WIDTH):
        tap = CONV_WIDTH - 1 - j
        conv = conv + w[tap : tap + 1] * zs_ref[pl.ds(HALO_ROWS - j, ts), :]
    o_ref[0] = (b_ref[0].astype(F32) * conv).astype(o_ref.dtype)


def _conv_gate(proj, wconv, B, S, D, *, ts=512, tc=1024):
    ts, tc = _tile(S, ts), _tile(D, tc)
    assert ts % HALO_ROWS == 0 and CONV_WIDTH - 1 <= HALO_ROWS
    nct = D // tc
    cur = lambda part: pl.BlockSpec((1, ts, tc), lambda b, i, j: (b, i, part * nct + j))
    halo = lambda part: pl.BlockSpec(
        (1, HALO_ROWS, tc), lambda b, i, j: (b, jnp.maximum(i * (ts // HALO_ROWS) - 1, 0), part * nct + j)
    )
    return pl.pallas_call(
        functools.partial(_convgate_body, ts=ts),
        grid=(B, S // ts, nct),
        in_specs=[cur(0), cur(1), cur(2), halo(1), halo(2), pl.BlockSpec((CONV_WIDTH, tc), lambda b, i, j: (0, j))],
        out_specs=pl.BlockSpec((1, ts, tc), lambda b, i, j: (b, i, j)),
        out_shape=jax.ShapeDtypeStruct((B, S, D), BF16),
        scratch_shapes=[pltpu.VMEM((ts + HALO_ROWS, tc), F32)],
        compiler_params=_params("parallel", "parallel", "parallel"),
        name="conv_gate",
    )(proj, proj, proj, proj, proj, wconv.astype(F32))


def _mixer_conv(xf, xb, win, wconv, wout, layer, alpha, B, S, D):
    proj = _matmul(xb, win, layer, n_cols=3 * D, out_dtype=BF16)
    u = _conv_gate(proj.reshape(B, S, 3 * D), wconv[layer], B, S, D)
    return _matmul(u.reshape(B * S, D), wout, layer, n_cols=D, out_dtype=F32, residual=xf, alpha=alpha)


def kernel(x, ln_g, ln_b, mlp_w1, mlp_w2, a_wqkv, a_wo, b_wgrp, b_scale, c_win, c_bf, c_wo, d_win, d_conv, d_wout):
    B, S, D = x.shape
    depth = ln_g.shape[0]
    n_mixers = 4
    alpha = (2.0 * depth) ** 0.25
    xf = x.reshape(B * S, D).astype(F32)
    xb = xf.astype(BF16)
    for i in range(depth):
        kind, j = i % n_mixers, i // n_mixers
        if kind == 1:
            xf, xb = _mixer_pool_ln(xf, b_wgrp[j], b_scale[j], ln_g[i, 0], ln_b[i, 0], alpha, B, S, D)
        else:
            if kind == 0 and i == 0:
                v, (w1b,) = _mixer_dilated(xf, xb, a_wqkv, a_wo, j, alpha, B, S, D, narrow=[(mlp_w1, 0)])
            elif kind == 0:
                v = _mixer_dilated(xf, xb, a_wqkv, a_wo, j, alpha, B, S, D)
            elif kind == 2:
                v = _mixer_fox(xf, xb, c_win, c_bf, c_wo, j, alpha, B, S, D)
            else:
                v = _mixer_conv(xf, xb, d_win, d_conv, d_wout, j, alpha, B, S, D)
            xf, xb = _layer_norm(v, ln_g[i, 0], ln_b[i, 0])
        sides = [(mlp_w2, i)] + ([(mlp_w1, i + 1)] if i + 1 < depth else [])
        hid, narrowed = _matmul(xb, w1b[None], 0, n_cols=w1b.shape[1], out_dtype=BF16, act="relu2", narrow=sides)
        v = _matmul_bf16w(hid, narrowed[0][None], 0, xf, alpha)
        w1b = narrowed[1] if i + 1 < depth else None
        xf, xb = _layer_norm(v, ln_g[i, 1], ln_b[i, 1])
    return xf.reshape(B, S, D).astype(x.dtype)
```

```python
import functools

import jax
import jax.numpy as jnp
from jax import lax
from jax.experimental import pallas as pl
from jax.experimental.pallas import tpu as pltpu

N_HEADS = 32
ROPE_THETA = 10000.0
DILATED_PATTERNS = ((128, 1), (512, 4), (2048, 16))
POOL_WINDOWS = (2, 4, 8, 16)
CONV_WIDTH = 3
LN_EPS = 1e-5
LANES = 128
HALO_ROWS = 16
VMEM_LIMIT_BYTES = 56 * 1024 * 1024
VMEM_LIMIT_LARGE_BYTES = 60000 * 1024
MASK_VALUE = -1e30
LOG2_E = 1.4426950408889634

BF16 = jnp.bfloat16
F32 = jnp.float32


def _tile(n, pref):
    t = min(n, pref)
    while n % t:
        assert t % 2 == 0, (n, pref)
        t //= 2
    return t


def _params(*sem, vmem_limit_bytes=VMEM_LIMIT_BYTES):
    return pltpu.CompilerParams(dimension_semantics=sem, vmem_limit_bytes=vmem_limit_bytes)


def _mm_acc_body(x_ref, w_ref, r_ref, o_ref, *, alpha):
    acc = jnp.dot(x_ref[...], w_ref[...], preferred_element_type=F32)
    k = pl.program_id(2)

    @pl.when(k == 0)
    def _():
        o_ref[...] = alpha * r_ref[...] + acc

    @pl.when(k > 0)
    def _():
        o_ref[...] += acc


def _matmul_bf16w(x, w, layer, residual, alpha, *, bm=1024, bn=1024, bk=4096):
    M, K = x.shape
    N = w.shape[2]
    bm, bn, bk = _tile(M, bm), _tile(N, bn), _tile(K, bk)
    out_spec = pl.BlockSpec((bm, bn), lambda i, j, k: (i, j))
    return pl.pallas_call(
        functools.partial(_mm_acc_body, alpha=alpha),
        grid=(M // bm, N // bn, K // bk),
        in_specs=[
            pl.BlockSpec((bm, bk), lambda i, j, k: (i, k)),
            pl.BlockSpec((None, bk, bn), lambda i, j, k: (layer, k, j)),
            out_spec,
        ],
        out_specs=out_spec,
        out_shape=jax.ShapeDtypeStruct((M, N), F32),
        compiler_params=_params("parallel", "parallel", "arbitrary", vmem_limit_bytes=VMEM_LIMIT_LARGE_BYTES),
        name="matmul_ktiled",
    )(x, w, residual)


def _mm_body(*refs, act, has_scale, rope, alpha, n_side, narrow_w):
    x_ref, w_ref = refs[0], refs[1]
    n_extra = 2 if rope else (1 if has_scale else 0)
    n_in = 2 + n_extra + (alpha is not None) + n_side
    o_ref = refs[n_in]
    for t in range(n_side):
        refs[n_in + 1 + t][...] = refs[n_in - n_side + t][...].astype(BF16)

    if narrow_w:
        wb_ref = refs[-1]

        @pl.when(pl.program_id(1) == 0)
        def _():
            wb_ref[...] = w_ref[...].astype(BF16)

        w_ref = wb_ref

    acc = jnp.dot(x_ref[...], w_ref[...], preferred_element_type=F32)
    if rope:
        cos, sin = refs[2][0], refs[3][0]
        for h in range(acc.shape[1] // LANES):
            cs = slice(h * LANES, (h + 1) * LANES)
            a = acc[:, cs]
            o_ref[:, cs] = (a * cos + pltpu.roll(a, LANES // 2, 1) * sin).astype(o_ref.dtype)
        return
    if act == "relu2":
        acc = jnp.maximum(acc, 0.0)
        acc = acc * acc
    if has_scale:
        acc = acc * refs[2][...]
    if alpha is not None:
        acc = alpha * refs[2 + n_extra][...] + acc
    o_ref[...] = acc.astype(o_ref.dtype)


def _matmul(x, w, layer, *, n_cols, out_dtype, col_offset=0, act=None, col_scale=None, rope=None, residual=None,
            alpha=None, narrow=(), bm=1024, bn=None):
    M, K = x.shape
    N = n_cols
    narrow_w = w.dtype != BF16
    if bn is None:
        bn = 512 if narrow_w else 1024
    if rope is not None:
        cos_tab, sin_tab, seq = rope
        bm, bn = _tile(seq, bm), _tile(N // 2, bn)
    else:
        bm, bn = _tile(M, bm), _tile(N, bn)
    assert col_offset % bn == 0 and (residual is None) == (alpha is None)
    off = col_offset // bn
    n_i = M // bm
    n_steps = (N // bn) * n_i
    out_spec = pl.BlockSpec((bm, bn), lambda j, i: (i, j))
    in_specs = [
        pl.BlockSpec((bm, K), lambda j, i: (i, 0)),
        pl.BlockSpec((None, K, bn), lambda j, i: (layer, 0, j + off)),
    ]
    args = [x, w]
    if rope is not None:
        tiles_per_part, seq_tiles = (N // 2) // bn, seq // bm
        tab_spec = pl.BlockSpec((1, bm, LANES), lambda j, i: (j // tiles_per_part, i % seq_tiles, 0))
        in_specs += [tab_spec, tab_spec]
        args += [cos_tab, sin_tab]
    elif col_scale is not None:
        in_specs.append(pl.BlockSpec((1, bn), lambda j, i: (0, j)))
        args.append(col_scale.reshape(1, N).astype(F32))
    if residual is not None:
        in_specs.append(out_spec)
        args.append(residual)
    out_specs, out_shape = [out_spec], [jax.ShapeDtypeStruct((M, N), out_dtype)]
    for src, src_layer in narrow:
        rows, width = src.shape[1], src.shape[2]
        slab = rows // n_steps
        assert slab * n_steps == rows and slab % HALO_ROWS == 0
        in_specs.append(pl.BlockSpec((None, slab, width), lambda j, i, src_layer=src_layer: (src_layer, j * n_i + i, 0)))
        args.append(src)
        out_specs.append(pl.BlockSpec((slab, width), lambda j, i: (j * n_i + i, 0)))
        out_shape.append(jax.ShapeDtypeStruct((rows, width), BF16))
    outs = pl.pallas_call(
        functools.partial(
            _mm_body, act=act, has_scale=col_scale is not None, rope=rope is not None, alpha=alpha,
            n_side=len(narrow), narrow_w=narrow_w,
        ),
        grid=(N // bn, n_i),
        in_specs=in_specs,
        out_specs=out_specs,
        out_shape=out_shape,
        scratch_shapes=[pltpu.VMEM((K, bn), BF16)] if narrow_w else [],
        compiler_params=_params("arbitrary", "arbitrary"),
        name="matmul",
    )(*args)
    return (outs[0], outs[1:]) if narrow else outs[0]


def _layer_norm_rows(v, g, b):
    mu = jnp.mean(v, axis=-1, keepdims=True)
    xc = v - mu
    var = jnp.mean(xc * xc, axis=-1, keepdims=True)
    return xc * lax.rsqrt(var + LN_EPS) * g + b


def _ln_body(v_ref, g_ref, b_ref, of_ref, ob_ref):
    out = _layer_norm_rows(v_ref[...], g_ref[...], b_ref[...])
    of_ref[...] = out
    ob_ref[...] = out.astype(BF16)


def _layer_norm(v, g, b, *, tm=256):
    M, D = v.shape
    tm = _tile(M, tm)
    row = pl.BlockSpec((tm, D), lambda i: (i, 0))
    vec = pl.BlockSpec((1, D), lambda i: (0, 0))
    return pl.pallas_call(
        _ln_body,
        grid=(M // tm,),
        in_specs=[row, vec, vec],
        out_specs=[row, row],
        out_shape=[jax.ShapeDtypeStruct((M, D), F32), jax.ShapeDtypeStruct((M, D), BF16)],
        compiler_params=_params("parallel"),
        name="layer_norm",
    )(v, g.reshape(1, D), b.reshape(1, D))


def _dilated_fused_body(q_ref, k_ref, v_ref, o_ref, q32, k32, v32, qc, kc, vc, ob, lb, *, patterns, unroll):
    S = q_ref.shape[1]
    q32[...] = q_ref[0].astype(F32)
    k32[...] = k_ref[0].astype(F32)
    v32[...] = v_ref[0].astype(F32)
    nt = (((1,), (1,)), ((), ()))
    for g, (window, d) in enumerate(patterns):
        blk = window // d
        sc = S // d
        n_blocks = sc // blk
        per_iter = min(n_blocks, unroll)
        classes = min(d, max(1, unroll // n_blocks))
        pad = sc + blk
        for c in range(classes):
            kc[c * pad : c * pad + blk, :] = jnp.zeros((blk, LANES), BF16)
            vc[c * pad : c * pad + blk, :] = jnp.zeros((blk, LANES), BF16)
        qi = lax.broadcasted_iota(jnp.int32, (blk, 2 * blk), 0)
        kj = lax.broadcasted_iota(jnp.int32, (blk, 2 * blk), 1)

        def class_body(rr, carry, g=g, d=d, blk=blk, sc=sc, n_blocks=n_blocks, per_iter=per_iter, classes=classes,
                       pad=pad, qi=qi, kj=kj):
            for c in range(classes):
                if d == 1:
                    qc[0:sc, :] = q_ref[0]
                    kc[blk:pad, :] = k_ref[0]
                    vc[blk:pad, :] = v_ref[0]
                else:
                    rows = pl.ds(rr * classes + c, sc, stride=d)
                    qc[c * sc : (c + 1) * sc, :] = q32[rows, :].astype(BF16)
                    kc[c * pad + blk : (c + 1) * pad, :] = k32[rows, :].astype(BF16)
                    vc[c * pad + blk : (c + 1) * pad, :] = v32[rows, :].astype(BF16)

            def blocks(it, carry2):
                units = [
                    (c, pl.multiple_of((it * per_iter + u) * blk, blk)) for c in range(classes) for u in range(per_iter)
                ]
                scores = [
                    lax.dot_general(
                        qc[pl.ds(c * sc + n0, blk), :],
                        kc[pl.ds(c * pad + n0, 2 * blk), :],
                        nt,
                        preferred_element_type=F32,
                    )
                    for c, n0 in units
                ]
                probs, dens, lses = [], [], []
                for (c, n0), s in zip(units, scores):
                    visible = (kj >= jnp.maximum(qi, blk - n0)) & (kj <= qi + blk)
                    s = jnp.where(visible, s, MASK_VALUE)
                    m = jnp.max(s, axis=1, keepdims=True)
                    p = jnp.exp(s - m)
                    den = jnp.sum(p, axis=1, keepdims=True)
                    probs.append(p.astype(BF16))
                    dens.append(den)
                    lses.append(m + jnp.log(den))
                for (c, n0), p, den, lse in zip(units, probs, dens, lses):
                    o = jnp.dot(p, vc[pl.ds(c * pad + n0, 2 * blk), :], preferred_element_type=F32) / den
                    out_rows = pl.ds(rr * classes + c + n0 * d, blk, stride=d) if d > 1 else pl.ds(n0, blk)
                    ob[g, out_rows, :] = o
                    lb[g, out_rows, :] = jnp.broadcast_to(lse, (blk, LANES))
                return carry2

            lax.fori_loop(0, n_blocks // per_iter, blocks, 0)
            return carry

        lax.fori_loop(0, d // classes, class_body, 0)

    merge_rows = min(S, 512)

    def merge(t, carry):
        rows = pl.ds(pl.multiple_of(t * merge_rows, merge_rows), merge_rows)
        lses = [lb[g, rows, :] for g in range(len(patterns))]
        m = functools.reduce(jnp.maximum, lses)
        es = [jnp.exp(l - m) for l in lses]
        tot = functools.reduce(lambda a, b: a + b, es)
        acc = functools.reduce(lambda a, b: a + b, [e * ob[g, rows, :] for g, e in enumerate(es)])
        o_ref[0, rows, :] = (acc / tot).astype(o_ref.dtype)
        return carry

    lax.fori_loop(0, S // merge_rows, merge, 0)


def _dilated_attention(qk, v, B, S, D, patterns, *, unroll=4):
    H = D // LANES
    for window, d in patterns:
        assert S % window == 0
    max_blk = max(w // d for w, d in patterns)
    head = lambda off: pl.BlockSpec((1, S, LANES), lambda b, h: (b, 0, off + h))
    return pl.pallas_call(
        functools.partial(_dilated_fused_body, patterns=patterns, unroll=unroll),
        grid=(B, H),
        in_specs=[head(0), head(H), head(0)],
        out_specs=head(0),
        out_shape=jax.ShapeDtypeStruct((B, S, D), BF16),
        scratch_shapes=[
            pltpu.VMEM((S, LANES), F32),
            pltpu.VMEM((S, LANES), F32),
            pltpu.VMEM((S, LANES), F32),
            pltpu.VMEM((S, LANES), BF16),
            pltpu.VMEM((S + unroll * max_blk, LANES), BF16),
            pltpu.VMEM((S + unroll * max_blk, LANES), BF16),
            pltpu.VMEM((len(patterns), S, LANES), F32),
            pltpu.VMEM((len(patterns), S, LANES), F32),
        ],
        compiler_params=_params("parallel", "parallel"),
        name="dilated_attention",
    )(qk.reshape(B, S, 2 * D), qk.reshape(B, S, 2 * D), v.reshape(B, S, D))


def _rope_tables(S, hd, q_scale):
    pos = jnp.arange(S, dtype=F32)
    inv = ROPE_THETA ** (-jnp.arange(0, hd, 2, dtype=F32) / hd)
    ang = pos[:, None] * inv[None, :]
    ang = jnp.concatenate([ang, ang], axis=-1)
    sign = jnp.concatenate([-jnp.ones((hd // 2,), F32), jnp.ones((hd // 2,), F32)])
    cos, sin = jnp.cos(ang), jnp.sin(ang) * sign
    return jnp.stack([cos * q_scale, cos]), jnp.stack([sin * q_scale, sin])


def _mixer_dilated(xf, xb, wqkv, wo, alpha, B, S, D, narrow=()):
    hd = D // N_HEADS
    assert hd == LANES
    cos_tab, sin_tab = _rope_tables(S, hd, hd**-0.5)
    qk = _matmul(xb, *wqkv, n_cols=2 * D, out_dtype=BF16, rope=(cos_tab, sin_tab, S), narrow=narrow)
    narrowed = ()
    if narrow:
        qk, narrowed = qk
    if wo is None:
        wo = (narrowed[-1][None], 0)
    v = _matmul(xb, *wqkv, n_cols=D, col_offset=2 * D, out_dtype=BF16)
    o = _dilated_attention(qk, v, B, S, D, DILATED_PATTERNS)
    out = _matmul(o.reshape(B * S, D), *wo, n_cols=D, out_dtype=F32, residual=xf, alpha=alpha)
    return out, narrowed


def _pool_ln_body(x_ref, xh_ref, w_ref, sc_ref, g_ref, b_ref, of_ref, ob_ref, xs_ref, y_ref, *, alpha, ts):
    i = pl.program_id(1)
    x = x_ref[0]
    xs_ref[0:HALO_ROWS, :] = jnp.where(i > 0, xh_ref[0], 0.0)
    xs_ref[HALO_ROWS:, :] = x
    cg = x.shape[1] // len(POOL_WINDOWS)
    t1 = (i * ts + 1 + lax.broadcasted_iota(jnp.int32, (ts, 1), 0)).astype(F32)
    for g, win in enumerate(POOL_WINDOWS):
        cs = slice(g * cg, (g + 1) * cg)
        tot = x[:, cs]
        for j in range(1, win):
            tot = tot + xs_ref[pl.ds(HALO_ROWS - j, ts), cs]
        pooled = tot / jnp.minimum(t1, float(win)) - x[:, cs]
        y = jnp.dot(pooled.astype(BF16), w_ref[g], preferred_element_type=F32)
        y_ref[:, cs] = y * sc_ref[:, cs]
    out = _layer_norm_rows(alpha * x + y_ref[...], g_ref[...], b_ref[...])
    of_ref[0] = out
    ob_ref[0] = out.astype(BF16)


def _mixer_pool_ln(xf, wgrp, scale, g, b, alpha, B, S, D, *, ts=256):
    ts = _tile(S, ts)
    assert ts % HALO_ROWS == 0 and max(POOL_WINDOWS) <= HALO_ROWS
    G, cg, _ = wgrp.shape
    x3 = xf.reshape(B, S, D)
    row = pl.BlockSpec((1, ts, D), lambda bb, i: (bb, i, 0))
    halo = pl.BlockSpec((1, HALO_ROWS, D), lambda bb, i: (bb, jnp.maximum(i * (ts // HALO_ROWS) - 1, 0), 0))
    vec = pl.BlockSpec((1, D), lambda bb, i: (0, 0))
    of, ob = pl.pallas_call(
        functools.partial(_pool_ln_body, alpha=alpha, ts=ts),
        grid=(B, S // ts),
        in_specs=[row, halo, pl.BlockSpec((G, cg, cg), lambda bb, i: (0, 0, 0)), vec, vec, vec],
        out_specs=[row, row],
        out_shape=[jax.ShapeDtypeStruct((B, S, D), F32), jax.ShapeDtypeStruct((B, S, D), BF16)],
        scratch_shapes=[pltpu.VMEM((ts + HALO_ROWS, D), F32), pltpu.VMEM((ts, D), F32)],
        compiler_params=_params("parallel", "parallel"),
        name="pool_ln",
    )(x3, x3, wgrp.astype(BF16), scale.reshape(1, D), g.reshape(1, D), b.reshape(1, D))
    return of.reshape(B * S, D), ob.reshape(B * S, D)


def _fgate_body(z_ref, bf_ref, c_ref):
    z = z_ref[0] + bf_ref[...]
    c = jnp.minimum(z, 0.0) - jnp.log1p(jnp.exp(-jnp.abs(z)))
    S = c.shape[0]
    row = lax.broadcasted_iota(jnp.int32, c.shape, 0)
    shift = 1
    while shift < S:
        c = c + jnp.where(row >= shift, pltpu.roll(c, shift, 0), 0.0)
        shift *= 2
    c_ref[0] = c


def _forget_cumsum(z, bias_row):
    B, S, W = z.shape
    blk = pl.BlockSpec((1, S, W), lambda b: (b, 0, 0))
    return pl.pallas_call(
        _fgate_body,
        grid=(B,),
        in_specs=[blk, pl.BlockSpec((1, W), lambda b: (0, 0))],
        out_specs=blk,
        out_shape=jax.ShapeDtypeStruct((B, S, W), F32),
        compiler_params=_params("parallel"),
        name="forget_cumsum",
    )(z, bias_row)


def _fox_body(q_ref, k_ref, v_ref, cc_ref, o_ref, qa_ref, ka_ref, *, bq, bk, heads):
    S = q_ref.shape[1]
    diag_steps = bq // bk
    lane = lax.broadcasted_iota(jnp.int32, (S, LANES), 1)
    for t in range(heads):
        head_lane = pl.program_id(1) * heads + t
        c = jnp.sum(jnp.where(lane == head_lane, cc_ref[0], 0.0), axis=1, keepdims=True) * LOG2_E
        hi = c.astype(BF16).astype(F32)
        mid = (c - hi).astype(BF16).astype(F32)
        lo = c - hi - mid

        def pieces(sign, first, hi=hi, mid=mid, lo=lo):
            return (
                jnp.where(lane == first, sign * hi, 0.0)
                + jnp.where(lane == first + 1, sign * mid, 0.0)
                + jnp.where(lane == first + 2, sign * lo, 0.0)
            )

        qa_ref[t] = (pieces(1.0, 0) + jnp.where((lane >= 3) & (lane < 6), 1.0, 0.0)).astype(BF16)
        ka_ref[t] = (jnp.where(lane < 3, 1.0, 0.0) + pieces(-1.0, 3)).astype(BF16)

    key_minus_query = lax.broadcasted_iota(jnp.int32, (bq, bk), 1) - lax.broadcasted_iota(jnp.int32, (bq, bk), 0)
    nt = (((1,), (1,)), ((), ()))
    across = lambda t: jnp.concatenate([t] * (bk // LANES), axis=1)
    cols = [slice(t * LANES, (t + 1) * LANES) for t in range(heads)]

    def q_block(i, carry):
        r0 = pl.multiple_of(i * bq, bq)
        rows = pl.ds(r0, bq)
        qs = [jnp.concatenate([q_ref[0, rows, cs], qa_ref[t, rows, :]], axis=1) for t, cs in enumerate(cols)]

        def k_step(c0, states, diag_offset):
            keys = pl.ds(c0, bk)
            scores = [
                lax.dot_general(
                    q, jnp.concatenate([k_ref[0, keys, cs], ka_ref[t, keys, :]], axis=1), nt, preferred_element_type=F32
                )
                for t, (q, cs) in enumerate(zip(qs, cols))
            ]
            new_states = []
            for s, (m, l, acc), cs in zip(scores, states, cols):
                if diag_offset is not None:
                    s = jnp.where(key_minus_query <= -diag_offset, s, MASK_VALUE)
                m_new = jnp.maximum(m, jnp.max(s, axis=1, keepdims=True))
                a = jnp.exp2(m - m_new)
                p = jnp.exp2(s - across(m_new))
                l = a * l + jnp.sum(p, axis=1, keepdims=True)
                acc = a * acc + jnp.dot(p.astype(BF16), v_ref[0, keys, cs], preferred_element_type=F32)
                new_states.append((m_new, l, acc))
            return tuple(new_states)

        init = (jnp.full((bq, LANES), MASK_VALUE, F32), jnp.zeros((bq, LANES), F32), jnp.zeros((bq, LANES), F32))
        states = lax.fori_loop(
            0, i * diag_steps, lambda j, st: k_step(pl.multiple_of(j * bk, bk), st, None), (init,) * heads
        )
        for jj in range(diag_steps):
            states = k_step(pl.multiple_of(r0 + jj * bk, bk), states, jj * bk)
        for (_, l, acc), cs in zip(states, cols):
            o_ref[0, rows, cs] = (acc / l).astype(o_ref.dtype)
        return carry

    lax.fori_loop(0, S // bq, q_block, 0)


def _fox_attention(proj, c_col, B, S, D, *, bq=1024, bk=1024, heads=2):
    bq = _tile(S, bq)
    bk = _tile(bq, bk)
    W = heads * LANES
    groups = D // W
    part = lambda p: pl.BlockSpec((1, S, W), lambda b, g: (b, 0, p * groups + g))
    return pl.pallas_call(
        functools.partial(_fox_body, bq=bq, bk=bk, heads=heads),
        grid=(B, groups),
        in_specs=[part(0), part(1), part(2), pl.BlockSpec((1, S, LANES), lambda b, g: (b, 0, 0))],
        out_specs=pl.BlockSpec((1, S, W), lambda b, g: (b, 0, g)),
        out_shape=jax.ShapeDtypeStruct((B, S, D), BF16),
        scratch_shapes=[pltpu.VMEM((heads, S, LANES), BF16), pltpu.VMEM((heads, S, LANES), BF16)],
        compiler_params=_params("parallel", "arbitrary"),
        name="fox_attention",
    )(proj, proj, proj, c_col)


def _mixer_fox(xf, xb, win, bf, wo, alpha, B, S, D):
    H = N_HEADS
    assert D // H == LANES and H <= LANES
    col_scale = jnp.concatenate([jnp.full((D,), LANES**-0.5 * LOG2_E, F32), jnp.ones((2 * D,), F32)])
    proj = _matmul(xb, *win, n_cols=3 * D, out_dtype=BF16, col_scale=col_scale)
    w_gate = jnp.pad(win[0][win[1], :, 3 * D :], ((0, 0), (0, LANES - H)))[None]
    z = _matmul(xb, w_gate, 0, n_cols=LANES, out_dtype=F32)
    bias_row = jnp.pad(bf.astype(F32), (0, LANES - H)).reshape(1, LANES)
    c_col = _forget_cumsum(z.reshape(B, S, LANES), bias_row)
    o = _fox_attention(proj.reshape(B, S, 3 * D), c_col, B, S, D)
    return _matmul(o.reshape(B * S, D), *wo, n_cols=D, out_dtype=F32, residual=xf, alpha=alpha)


def _convgate_body(b_ref, c_ref, h_ref, ch_ref, hh_ref, w_ref, o_ref, zs_ref, *, ts):
    i = pl.program_id(1)
    z = c_ref[0].astype(F32) * h_ref[0].astype(F32)
    zh = ch_ref[0].astype(F32) * hh_ref[0].astype(F32)
    zs_ref[0:HALO_ROWS, :] = jnp.where(i > 0, zh, 0.0)
    zs_ref[HALO_ROWS:, :] = z
    w = w_ref[...]
    conv = w[CONV_WIDTH - 1 : CONV_WIDTH] * z
    for j in range(1, CONV_WIDTH):
        tap = CONV_WIDTH - 1 - j
        conv = conv + w[tap : tap + 1] * zs_ref[pl.ds(HALO_ROWS - j, ts), :]
    o_ref[0] = (b_ref[0].astype(F32) * conv).astype(o_ref.dtype)


def _conv_gate(proj, wconv, B, S, D, *, ts=512, tc=1024):
    ts, tc = _tile(S, ts), _tile(D, tc)
    assert ts % HALO_ROWS == 0 and CONV_WIDTH - 1 <= HALO_ROWS
    nct = D // tc
    cur = lambda part: pl.BlockSpec((1, ts, tc), lambda b, i, j: (b, i, part * nct + j))
    halo = lambda part: pl.BlockSpec(
        (1, HALO_ROWS, tc), lambda b, i, j: (b, jnp.maximum(i * (ts // HALO_ROWS) - 1, 0), part * nct + j)
    )
    return pl.pallas_call(
        functools.partial(_convgate_body, ts=ts),
        grid=(B, S // ts, nct),
        in_specs=[cur(0), cur(1), cur(2), halo(1), halo(2), pl.BlockSpec((CONV_WIDTH, tc), lambda b, i, j: (0, j))],
        out_specs=pl.BlockSpec((1, ts, tc), lambda b, i, j: (b, i, j)),
        out_shape=jax.ShapeDtypeStruct((B, S, D), BF16),
        scratch_shapes=[pltpu.VMEM((ts + HALO_ROWS, tc), F32)],
        compiler_params=_params("parallel", "parallel", "parallel"),
        name="conv_gate",
    )(proj, proj, proj, proj, proj, wconv.astype(F32))


def _mixer_conv(xf, xb, win, wconv, wout, alpha, B, S, D):
    proj = _matmul(xb, *win, n_cols=3 * D, out_dtype=BF16)
    u = _conv_gate(proj.reshape(B, S, 3 * D), wconv, B, S, D)
    return _matmul(u.reshape(B * S, D), *wout, n_cols=D, out_dtype=F32, residual=xf, alpha=alpha)


def kernel(x, ln_g, ln_b, mlp_w1, mlp_w2, a_wqkv, a_wo, b_wgrp, b_scale, c_win, c_bf, c_wo, d_win, d_conv, d_wout):
    B, S, D = x.shape
    depth = ln_g.shape[0]
    n_mixers = 4
    alpha = (2.0 * depth) ** 0.25
    xf = x.reshape(B * S, D).astype(F32)
    xb = xf.astype(BF16)

    mixer_weights = {0: (a_wqkv, a_wo), 2: (c_win, c_wo), 3: (d_win, d_wout)}
    narrowed = {}

    def upcoming(i):
        if i >= depth:
            return []
        mats = [(w, i // n_mixers) for w in mixer_weights.get(i % n_mixers, ())]
        return mats + [(mlp_w1, i)]

    def weight(w, layer):
        b = narrowed.get((id(w), layer))
        return (w, layer) if b is None else (b[None], 0)

    def record(sides, results):
        for (w, layer), b in zip(sides, results):
            narrowed[(id(w), layer)] = b

    for i in range(depth):
        kind, j = i % n_mixers, i // n_mixers
        if kind == 1:
            xf, xb = _mixer_pool_ln(xf, b_wgrp[j], b_scale[j], ln_g[i, 0], ln_b[i, 0], alpha, B, S, D)
        else:
            if kind == 0:
                sides = [s for s in [(mlp_w1, i), (a_wo, j)] if (id(s[0]), s[1]) not in narrowed]
                wo = None if sides and sides[-1][0] is a_wo else weight(a_wo, j)
                v, results = _mixer_dilated(xf, xb, weight(a_wqkv, j), wo, alpha, B, S, D, narrow=sides)
                record(sides, results)
            elif kind == 2:
                v = _mixer_fox(xf, xb, weight(c_win, j), c_bf[j], weight(c_wo, j), alpha, B, S, D)
            else:
                v = _mixer_conv(xf, xb, weight(d_win, j), d_conv[j], weight(d_wout, j), alpha, B, S, D)
            xf, xb = _layer_norm(v, ln_g[i, 0], ln_b[i, 0])
        sides = [(mlp_w2, i)] + upcoming(i + 1)
        hid, results = _matmul(xb, *weight(mlp_w1, i), n_cols=mlp_w1.shape[2], out_dtype=BF16, act="relu2", narrow=sides)
        record(sides, results)
        v = _matmul_bf16w(hid, *weight(mlp_w2, i), xf, alpha)
        xf, xb = _layer_norm(v, ln_g[i, 1], ln_b[i, 1])
    return xf.reshape(B, S, D).astype(x.dtype)
```

```python
import functools

import jax
import jax.numpy as jnp
from jax import lax
from jax.experimental import pallas as pl
from jax.experimental.pallas import tpu as pltpu

N_HEADS = 32
ROPE_THETA = 10000.0
DILATED_PATTERNS = ((128, 1), (512, 4), (2048, 16))
POOL_WINDOWS = (2, 4, 8, 16)
CONV_WIDTH = 3
LN_EPS = 1e-5
LANES = 128
MXU_WIDTH = 256
HALO_ROWS = 16
VMEM_LIMIT_BYTES = 56 * 1024 * 1024
VMEM_LIMIT_LARGE_BYTES = 60000 * 1024
MASK_VALUE = -1e30
LOG2_E = 1.4426950408889634

BF16 = jnp.bfloat16
F32 = jnp.float32


def _tile(n, pref):
    t = min(n, pref)
    while n % t:
        assert t % 2 == 0, (n, pref)
        t //= 2
    return t


def _params(*sem, vmem_limit_bytes=VMEM_LIMIT_BYTES):
    return pltpu.CompilerParams(dimension_semantics=sem, vmem_limit_bytes=vmem_limit_bytes)


def _mm_acc_body(x_ref, w_ref, r_ref, o_ref, *, alpha):
    k = pl.program_id(2)
    x = x_ref[...]
    width = min(o_ref.shape[1], MXU_WIDTH)
    for c in range(o_ref.shape[1] // width):
        cs = slice(c * width, (c + 1) * width)
        acc = jnp.dot(x, w_ref[:, cs], preferred_element_type=F32)
        o_ref[:, cs] = jnp.where(k == 0, alpha * r_ref[:, cs], o_ref[:, cs]) + acc


def _matmul_bf16w(x, w, layer, residual, alpha, *, bm=1024, bn=1024, bk=4096):
    M, K = x.shape
    N = w.shape[2]
    bm, bn, bk = _tile(M, bm), _tile(N, bn), _tile(K, bk)
    out_spec = pl.BlockSpec((bm, bn), lambda i, j, k: (i, j))
    return pl.pallas_call(
        functools.partial(_mm_acc_body, alpha=alpha),
        grid=(M // bm, N // bn, K // bk),
        in_specs=[
            pl.BlockSpec((bm, bk), lambda i, j, k: (i, k)),
            pl.BlockSpec((None, bk, bn), lambda i, j, k: (layer, k, j)),
            out_spec,
        ],
        out_specs=out_spec,
        out_shape=jax.ShapeDtypeStruct((M, N), F32),
        compiler_params=_params("parallel", "parallel", "arbitrary", vmem_limit_bytes=VMEM_LIMIT_LARGE_BYTES),
        name="matmul_ktiled",
    )(x, w, residual)


def _mm_body(*refs, act, has_scale, rope, alpha, n_side, narrow_w):
    x_ref, w_ref = refs[0], refs[1]
    n_extra = 2 if rope else (1 if has_scale else 0)
    n_in = 2 + n_extra + (alpha is not None) + n_side
    o_ref = refs[n_in]
    for t in range(n_side):
        refs[n_in + 1 + t][...] = refs[n_in - n_side + t][...].astype(BF16)

    if narrow_w:
        wb_ref = refs[-1]

        @pl.when(pl.program_id(1) == 0)
        def _():
            wb_ref[...] = w_ref[...].astype(BF16)

        w_ref = wb_ref

    acc = jnp.dot(x_ref[...], w_ref[...], preferred_element_type=F32)
    if rope:
        cos, sin = refs[2][0], refs[3][0]
        for h in range(acc.shape[1] // LANES):
            cs = slice(h * LANES, (h + 1) * LANES)
            a = acc[:, cs]
            o_ref[:, cs] = (a * cos + pltpu.roll(a, LANES // 2, 1) * sin).astype(o_ref.dtype)
        return
    if act == "relu2":
        acc = jnp.maximum(acc, 0.0)
        acc = acc * acc
    if has_scale:
        acc = acc * refs[2][...]
    if alpha is not None:
        acc = alpha * refs[2 + n_extra][...] + acc
    o_ref[...] = acc.astype(o_ref.dtype)


def _matmul(x, w, layer, *, n_cols, out_dtype, col_offset=0, act=None, col_scale=None, rope=None, residual=None,
            alpha=None, narrow=(), bm=1024, bn=None):
    M, K = x.shape
    N = n_cols
    narrow_w = w.dtype != BF16
    if bn is None:
        bn = 512 if narrow_w else 1024
    if rope is not None:
        cos_tab, sin_tab, seq = rope
        bm, bn = _tile(seq, bm), _tile(N // 2, bn)
    else:
        bm, bn = _tile(M, bm), _tile(N, bn)
    assert col_offset % bn == 0 and (residual is None) == (alpha is None)
    off = col_offset // bn
    n_i = M // bm
    n_steps = (N // bn) * n_i
    out_spec = pl.BlockSpec((bm, bn), lambda j, i: (i, j))
    in_specs = [
        pl.BlockSpec((bm, K), lambda j, i: (i, 0)),
        pl.BlockSpec((None, K, bn), lambda j, i: (layer, 0, j + off)),
    ]
    args = [x, w]
    if rope is not None:
        tiles_per_part, seq_tiles = (N // 2) // bn, seq // bm
        tab_spec = pl.BlockSpec((1, bm, LANES), lambda j, i: (j // tiles_per_part, i % seq_tiles, 0))
        in_specs += [tab_spec, tab_spec]
        args += [cos_tab, sin_tab]
    elif col_scale is not None:
        in_specs.append(pl.BlockSpec((1, bn), lambda j, i: (0, j)))
        args.append(col_scale.reshape(1, N).astype(F32))
    if residual is not None:
        in_specs.append(out_spec)
        args.append(residual)
    out_specs, out_shape = [out_spec], [jax.ShapeDtypeStruct((M, N), out_dtype)]
    for src, src_layer in narrow:
        rows, width = src.shape[1], src.shape[2]
        slab = rows // n_steps
        assert slab * n_steps == rows and slab % HALO_ROWS == 0
        in_specs.append(pl.BlockSpec((None, slab, width), lambda j, i, src_layer=src_layer: (src_layer, j * n_i + i, 0)))
        args.append(src)
        out_specs.append(pl.BlockSpec((slab, width), lambda j, i: (j * n_i + i, 0)))
        out_shape.append(jax.ShapeDtypeStruct((rows, width), BF16))
    outs = pl.pallas_call(
        functools.partial(
            _mm_body, act=act, has_scale=col_scale is not None, rope=rope is not None, alpha=alpha,
            n_side=len(narrow), narrow_w=narrow_w,
        ),
        grid=(N // bn, n_i),
        in_specs=in_specs,
        out_specs=out_specs,
        out_shape=out_shape,
        scratch_shapes=[pltpu.VMEM((K, bn), BF16)] if narrow_w else [],
        compiler_params=_params("arbitrary", "arbitrary"),
        name="matmul",
    )(*args)
    return (outs[0], outs[1:]) if narrow else outs[0]


def _layer_norm_rows(v, g, b):
    mu = jnp.mean(v, axis=-1, keepdims=True)
    xc = v - mu
    var = jnp.mean(xc * xc, axis=-1, keepdims=True)
    return xc * lax.rsqrt(var + LN_EPS) * g + b


def _ln_body(v_ref, g_ref, b_ref, of_ref, ob_ref):
    out = _layer_norm_rows(v_ref[...], g_ref[...], b_ref[...])
    of_ref[...] = out
    ob_ref[...] = out.astype(BF16)


def _layer_norm(v, g, b, *, tm=256):
    M, D = v.shape
    tm = _tile(M, tm)
    row = pl.BlockSpec((tm, D), lambda i: (i, 0))
    vec = pl.BlockSpec((1, D), lambda i: (0, 0))
    return pl.pallas_call(
        _ln_body,
        grid=(M // tm,),
        in_specs=[row, vec, vec],
        out_specs=[row, row],
        out_shape=[jax.ShapeDtypeStruct((M, D), F32), jax.ShapeDtypeStruct((M, D), BF16)],
        compiler_params=_params("parallel"),
        name="layer_norm",
    )(v, g.reshape(1, D), b.reshape(1, D))


def _dilated_fused_body(q_ref, k_ref, v_ref, o_ref, q32, k32, v32, qc, kc, vc, ob, lb, *, patterns, unroll):
    S = q_ref.shape[1]
    q32[...] = q_ref[0].astype(F32)
    k32[...] = k_ref[0].astype(F32)
    v32[...] = v_ref[0].astype(F32)
    nt = (((1,), (1,)), ((), ()))
    for g, (window, d) in enumerate(patterns):
        blk = window // d
        sc = S // d
        n_blocks = sc // blk
        per_iter = min(n_blocks, unroll)
        classes = min(d, max(1, unroll // n_blocks))
        pad = sc + blk
        for c in range(classes):
            kc[c * pad : c * pad + blk, :] = jnp.zeros((blk, LANES), BF16)
            vc[c * pad : c * pad + blk, :] = jnp.zeros((blk, LANES), BF16)
        qi = lax.broadcasted_iota(jnp.int32, (blk, 2 * blk), 0)
        kj = lax.broadcasted_iota(jnp.int32, (blk, 2 * blk), 1)

        def class_body(rr, carry, g=g, d=d, blk=blk, sc=sc, n_blocks=n_blocks, per_iter=per_iter, classes=classes,
                       pad=pad, qi=qi, kj=kj):
            for c in range(classes):
                if d == 1:
                    qc[0:sc, :] = q_ref[0]
                    kc[blk:pad, :] = k_ref[0]
                    vc[blk:pad, :] = v_ref[0]
                else:
                    rows = pl.ds(rr * classes + c, sc, stride=d)
                    qc[c * sc : (c + 1) * sc, :] = q32[rows, :].astype(BF16)
                    kc[c * pad + blk : (c + 1) * pad, :] = k32[rows, :].astype(BF16)
                    vc[c * pad + blk : (c + 1) * pad, :] = v32[rows, :].astype(BF16)

            def blocks(it, carry2):
                units = [
                    (c, pl.multiple_of((it * per_iter + u) * blk, blk)) for c in range(classes) for u in range(per_iter)
                ]
                scores = [
                    lax.dot_general(
                        qc[pl.ds(c * sc + n0, blk), :],
                        kc[pl.ds(c * pad + n0, 2 * blk), :],
                        nt,
                        preferred_element_type=F32,
                    )
                    for c, n0 in units
                ]
                probs, dens, lses = [], [], []
                for (c, n0), s in zip(units, scores):
                    visible = (kj >= jnp.maximum(qi, blk - n0)) & (kj <= qi + blk)
                    s = jnp.where(visible, s, MASK_VALUE)
                    m = jnp.max(s, axis=1, keepdims=True)
                    p = jnp.exp(s - m)
                    den = jnp.sum(p, axis=1, keepdims=True)
                    probs.append(p.astype(BF16))
                    dens.append(den)
                    lses.append(m + jnp.log(den))
                for (c, n0), p, den, lse in zip(units, probs, dens, lses):
                    o = jnp.dot(p, vc[pl.ds(c * pad + n0, 2 * blk), :], preferred_element_type=F32) / den
                    out_rows = pl.ds(rr * classes + c + n0 * d, blk, stride=d) if d > 1 else pl.ds(n0, blk)
                    ob[g, out_rows, :] = o
                    lb[g, out_rows, :] = jnp.broadcast_to(lse, (blk, LANES))
                return carry2

            lax.fori_loop(0, n_blocks // per_iter, blocks, 0)
            return carry

        lax.fori_loop(0, d // classes, class_body, 0)

    merge_rows = min(S, 512)

    def merge(t, carry):
        rows = pl.ds(pl.multiple_of(t * merge_rows, merge_rows), merge_rows)
        lses = [lb[g, rows, :] for g in range(len(patterns))]
        m = functools.reduce(jnp.maximum, lses)
        es = [jnp.exp(l - m) for l in lses]
        tot = functools.reduce(lambda a, b: a + b, es)
        acc = functools.reduce(lambda a, b: a + b, [e * ob[g, rows, :] for g, e in enumerate(es)])
        o_ref[0, rows, :] = (acc / tot).astype(o_ref.dtype)
        return carry

    lax.fori_loop(0, S // merge_rows, merge, 0)


def _dilated_attention(qk, v, B, S, D, patterns, *, unroll=4):
    H = D // LANES
    for window, d in patterns:
        assert S % window == 0
    max_blk = max(w // d for w, d in patterns)
    head = lambda off: pl.BlockSpec((1, S, LANES), lambda b, h: (b, 0, off + h))
    return pl.pallas_call(
        functools.partial(_dilated_fused_body, patterns=patterns, unroll=unroll),
        grid=(B, H),
        in_specs=[head(0), head(H), head(0)],
        out_specs=head(0),
        out_shape=jax.ShapeDtypeStruct((B, S, D), BF16),
        scratch_shapes=[
            pltpu.VMEM((S, LANES), F32),
            pltpu.VMEM((S, LANES), F32),
            pltpu.VMEM((S, LANES), F32),
            pltpu.VMEM((S, LANES), BF16),
            pltpu.VMEM((S + unroll * max_blk, LANES), BF16),
            pltpu.VMEM((S + unroll * max_blk, LANES), BF16),
            pltpu.VMEM((len(patterns), S, LANES), F32),
            pltpu.VMEM((len(patterns), S, LANES), F32),
        ],
        compiler_params=_params("parallel", "parallel"),
        name="dilated_attention",
    )(qk.reshape(B, S, 2 * D), qk.reshape(B, S, 2 * D), v.reshape(B, S, D))


def _rope_tables(S, hd, q_scale):
    pos = jnp.arange(S, dtype=F32)
    inv = ROPE_THETA ** (-jnp.arange(0, hd, 2, dtype=F32) / hd)
    ang = pos[:, None] * inv[None, :]
    ang = jnp.concatenate([ang, ang], axis=-1)
    sign = jnp.concatenate([-jnp.ones((hd // 2,), F32), jnp.ones((hd // 2,), F32)])
    cos, sin = jnp.cos(ang), jnp.sin(ang) * sign
    return jnp.stack([cos * q_scale, cos]), jnp.stack([sin * q_scale, sin])


def _mixer_dilated(xf, xb, wqkv, wo, alpha, B, S, D, narrow=()):
    hd = D // N_HEADS
    assert hd == LANES
    cos_tab, sin_tab = _rope_tables(S, hd, hd**-0.5)
    qk = _matmul(xb, *wqkv, n_cols=2 * D, out_dtype=BF16, rope=(cos_tab, sin_tab, S), narrow=narrow)
    narrowed = ()
    if narrow:
        qk, narrowed = qk
    if wo is None:
        wo = (narrowed[-1][None], 0)
    v = _matmul(xb, *wqkv, n_cols=D, col_offset=2 * D, out_dtype=BF16)
    o = _dilated_attention(qk, v, B, S, D, DILATED_PATTERNS)
    out = _matmul(o.reshape(B * S, D), *wo, n_cols=D, out_dtype=F32, residual=xf, alpha=alpha)
    return out, narrowed


def _pool_ln_body(x_ref, xh_ref, w_ref, sc_ref, g_ref, b_ref, of_ref, ob_ref, xs_ref, y_ref, *, alpha, ts):
    i = pl.program_id(1)
    x = x_ref[0]
    xs_ref[0:HALO_ROWS, :] = jnp.where(i > 0, xh_ref[0], 0.0)
    xs_ref[HALO_ROWS:, :] = x
    cg = x.shape[1] // len(POOL_WINDOWS)
    t1 = (i * ts + 1 + lax.broadcasted_iota(jnp.int32, (ts, 1), 0)).astype(F32)
    for g, win in enumerate(POOL_WINDOWS):
        cs = slice(g * cg, (g + 1) * cg)
        tot = x[:, cs]
        for j in range(1, win):
            tot = tot + xs_ref[pl.ds(HALO_ROWS - j, ts), cs]
        pooled = tot / jnp.minimum(t1, float(win)) - x[:, cs]
        y = jnp.dot(pooled.astype(BF16), w_ref[g], preferred_element_type=F32)
        y_ref[:, cs] = y * sc_ref[:, cs]
    out = _layer_norm_rows(alpha * x + y_ref[...], g_ref[...], b_ref[...])
    of_ref[0] = out
    ob_ref[0] = out.astype(BF16)


def _mixer_pool_ln(xf, wgrp, scale, g, b, alpha, B, S, D, *, ts=256):
    ts = _tile(S, ts)
    assert ts % HALO_ROWS == 0 and max(POOL_WINDOWS) <= HALO_ROWS
    G, cg, _ = wgrp.shape
    x3 = xf.reshape(B, S, D)
    row = pl.BlockSpec((1, ts, D), lambda bb, i: (bb, i, 0))
    halo = pl.BlockSpec((1, HALO_ROWS, D), lambda bb, i: (bb, jnp.maximum(i * (ts // HALO_ROWS) - 1, 0), 0))
    vec = pl.BlockSpec((1, D), lambda bb, i: (0, 0))
    of, ob = pl.pallas_call(
        functools.partial(_pool_ln_body, alpha=alpha, ts=ts),
        grid=(B, S // ts),
        in_specs=[row, halo, pl.BlockSpec((G, cg, cg), lambda bb, i: (0, 0, 0)), vec, vec, vec],
        out_specs=[row, row],
        out_shape=[jax.ShapeDtypeStruct((B, S, D), F32), jax.ShapeDtypeStruct((B, S, D), BF16)],
        scratch_shapes=[pltpu.VMEM((ts + HALO_ROWS, D), F32), pltpu.VMEM((ts, D), F32)],
        compiler_params=_params("parallel", "parallel"),
        name="pool_ln",
    )(x3, x3, wgrp.astype(BF16), scale.reshape(1, D), g.reshape(1, D), b.reshape(1, D))
    return of.reshape(B * S, D), ob.reshape(B * S, D)


def _fgate_body(z_ref, bf_ref, c_ref):
    z = z_ref[0] + bf_ref[...]
    c = jnp.minimum(z, 0.0) - jnp.log1p(jnp.exp(-jnp.abs(z)))
    S = c.shape[0]
    row = lax.broadcasted_iota(jnp.int32, c.shape, 0)
    shift = 1
    while shift < S:
        c = c + jnp.where(row >= shift, pltpu.roll(c, shift, 0), 0.0)
        shift *= 2
    c_ref[0] = c


def _forget_cumsum(z, bias_row):
    B, S, W = z.shape
    blk = pl.BlockSpec((1, S, W), lambda b: (b, 0, 0))
    return pl.pallas_call(
        _fgate_body,
        grid=(B,),
        in_specs=[blk, pl.BlockSpec((1, W), lambda b: (0, 0))],
        out_specs=blk,
        out_shape=jax.ShapeDtypeStruct((B, S, W), F32),
        compiler_params=_params("parallel"),
        name="forget_cumsum",
    )(z, bias_row)


def _fox_body(q_ref, k_ref, v_ref, cc_ref, o_ref, qa_ref, ka_ref, *, bq, bk, heads):
    S = q_ref.shape[1]
    diag_steps = bq // bk
    lane = lax.broadcasted_iota(jnp.int32, (S, LANES), 1)
    for t in range(heads):
        head_lane = pl.program_id(1) * heads + t
        c = jnp.sum(jnp.where(lane == head_lane, cc_ref[0], 0.0), axis=1, keepdims=True) * LOG2_E
        hi = c.astype(BF16).astype(F32)
        mid = (c - hi).astype(BF16).astype(F32)
        lo = c - hi - mid

        def pieces(sign, first, hi=hi, mid=mid, lo=lo):
            return (
                jnp.where(lane == first, sign * hi, 0.0)
                + jnp.where(lane == first + 1, sign * mid, 0.0)
                + jnp.where(lane == first + 2, sign * lo, 0.0)
            )

        qa_ref[t] = (pieces(1.0, 0) + jnp.where((lane >= 3) & (lane < 6), 1.0, 0.0)).astype(BF16)
        ka_ref[t] = (jnp.where(lane < 3, 1.0, 0.0) + pieces(-1.0, 3)).astype(BF16)

    key_minus_query = lax.broadcasted_iota(jnp.int32, (bq, bk), 1) - lax.broadcasted_iota(jnp.int32, (bq, bk), 0)
    nt = (((1,), (1,)), ((), ()))
    across = lambda t: jnp.concatenate([t] * (bk // LANES), axis=1)
    cols = [slice(t * LANES, (t + 1) * LANES) for t in range(heads)]

    def q_block(i, carry):
        r0 = pl.multiple_of(i * bq, bq)
        rows = pl.ds(r0, bq)
        qs = [jnp.concatenate([q_ref[0, rows, cs], qa_ref[t, rows, :]], axis=1) for t, cs in enumerate(cols)]

        def k_step(c0, states, diag_offset):
            keys = pl.ds(c0, bk)
            scores = [
                lax.dot_general(
                    q, jnp.concatenate([k_ref[0, keys, cs], ka_ref[t, keys, :]], axis=1), nt, preferred_element_type=F32
                )
                for t, (q, cs) in enumerate(zip(qs, cols))
            ]
            new_states = []
            for s, (m, l, acc), cs in zip(scores, states, cols):
                if diag_offset is not None:
                    s = jnp.where(key_minus_query <= -diag_offset, s, MASK_VALUE)
                m_new = jnp.maximum(m, jnp.max(s, axis=1, keepdims=True))
                a = jnp.exp2(m - m_new)
                p = jnp.exp2(s - across(m_new))
                l = a * l + jnp.sum(p, axis=1, keepdims=True)
                acc = a * acc + jnp.dot(p.astype(BF16), v_ref[0, keys, cs], preferred_element_type=F32)
                new_states.append((m_new, l, acc))
            return tuple(new_states)

        init = (jnp.full((bq, LANES), MASK_VALUE, F32), jnp.zeros((bq, LANES), F32), jnp.zeros((bq, LANES), F32))
        states = lax.fori_loop(
            0, i * diag_steps, lambda j, st: k_step(pl.multiple_of(j * bk, bk), st, None), (init,) * heads
        )
        for jj in range(diag_steps):
            states = k_step(pl.multiple_of(r0 + jj * bk, bk), states, jj * bk)
        for (_, l, acc), cs in zip(states, cols):
            o_ref[0, rows, cs] = (acc / l).astype(o_ref.dtype)
        return carry

    lax.fori_loop(0, S // bq, q_block, 0)


def _fox_attention(proj, c_col, B, S, D, *, bq=1024, bk=1024, heads=2):
    bq = _tile(S, bq)
    bk = _tile(bq, bk)
    W = heads * LANES
    groups = D // W
    part = lambda p: pl.BlockSpec((1, S, W), lambda b, g: (b, 0, p * groups + g))
    return pl.pallas_call(
        functools.partial(_fox_body, bq=bq, bk=bk, heads=heads),
        grid=(B, groups),
        in_specs=[part(0), part(1), part(2), pl.BlockSpec((1, S, LANES), lambda b, g: (b, 0, 0))],
        out_specs=pl.BlockSpec((1, S, W), lambda b, g: (b, 0, g)),
        out_shape=jax.ShapeDtypeStruct((B, S, D), BF16),
        scratch_shapes=[pltpu.VMEM((heads, S, LANES), BF16), pltpu.VMEM((heads, S, LANES), BF16)],
        compiler_params=_params("parallel", "arbitrary"),
        name="fox_attention",
    )(proj, proj, proj, c_col)


def _mixer_fox(xf, xb, win, bf, wo, alpha, B, S, D):
    H = N_HEADS
    assert D // H == LANES and H <= LANES
    col_scale = jnp.concatenate([jnp.full((D,), LANES**-0.5 * LOG2_E, F32), jnp.ones((2 * D,), F32)])
    proj = _matmul(xb, *win, n_cols=3 * D, out_dtype=BF16, col_scale=col_scale)
    w_gate = jnp.pad(win[0][win[1], :, 3 * D :], ((0, 0), (0, LANES - H)))[None]
    z = _matmul(xb, w_gate, 0, n_cols=LANES, out_dtype=F32)
    bias_row = jnp.pad(bf.astype(F32), (0, LANES - H)).reshape(1, LANES)
    c_col = _forget_cumsum(z.reshape(B, S, LANES), bias_row)
    o = _fox_attention(proj.reshape(B, S, 3 * D), c_col, B, S, D)
    return _matmul(o.reshape(B * S, D), *wo, n_cols=D, out_dtype=F32, residual=xf, alpha=alpha)


def _convgate_body(b_ref, c_ref, h_ref, ch_ref, hh_ref, w_ref, o_ref, zs_ref, *, ts):
    i = pl.program_id(1)
    z = c_ref[0].astype(F32) * h_ref[0].astype(F32)
    zh = ch_ref[0].astype(F32) * hh_ref[0].astype(F32)
    zs_ref[0:HALO_ROWS, :] = jnp.where(i > 0, zh, 0.0)
    zs_ref[HALO_ROWS:, :] = z
    w = w_ref[...]
    conv = w[CONV_WIDTH - 1 : CONV_WIDTH] * z
    for j in range(1, CONV_WIDTH):
        tap = CONV_WIDTH - 1 - j
        conv = conv + w[tap : tap + 1] * zs_ref[pl.ds(HALO_ROWS - j, ts), :]
    o_ref[0] = (b_ref[0].astype(F32) * conv).astype(o_ref.dtype)


def _conv_gate(proj, wconv, B, S, D, *, ts=512, tc=1024):
    ts, tc = _tile(S, ts), _tile(D, tc)
    assert ts % HALO_ROWS == 0 and CONV_WIDTH - 1 <= HALO_ROWS
    nct = D // tc
    cur = lambda part: pl.BlockSpec((1, ts, tc), lambda b, i, j: (b, i, part * nct + j))
    halo = lambda part: pl.BlockSpec(
        (1, HALO_ROWS, tc), lambda b, i, j: (b, jnp.maximum(i * (ts // HALO_ROWS) - 1, 0), part * nct + j)
    )
    return pl.pallas_call(
        functools.partial(_convgate_body, ts=ts),
        grid=(B, S // ts, nct),
        in_specs=[cur(0), cur(1), cur(2), halo(1), halo(2), pl.BlockSpec((CONV_WIDTH, tc), lambda b, i, j: (0, j))],
        out_specs=pl.BlockSpec((1, ts, tc), lambda b, i, j: (b, i, j)),
        out_shape=jax.ShapeDtypeStruct((B, S, D), BF16),
        scratch_shapes=[pltpu.VMEM((ts + HALO_ROWS, tc), F32)],
        compiler_params=_params("parallel", "parallel", "parallel"),
        name="conv_gate",
    )(proj, proj, proj, proj, proj, wconv.astype(F32))


def _mixer_conv(xf, xb, win, wconv, wout, alpha, B, S, D):
    proj = _matmul(xb, *win, n_cols=3 * D, out_dtype=BF16)
    u = _conv_gate(proj.reshape(B, S, 3 * D), wconv, B, S, D)
    return _matmul(u.reshape(B * S, D), *wout, n_cols=D, out_dtype=F32, residual=xf, alpha=alpha)


def kernel(x, ln_g, ln_b, mlp_w1, mlp_w2, a_wqkv, a_wo, b_wgrp, b_scale, c_win, c_bf, c_wo, d_win, d_conv, d_wout):
    B, S, D = x.shape
    depth = ln_g.shape[0]
    n_mixers = 4
    alpha = (2.0 * depth) ** 0.25
    xf = x.reshape(B * S, D).astype(F32)
    xb = xf.astype(BF16)

    mixer_weights = {0: (a_wqkv, a_wo), 2: (c_win, c_wo), 3: (d_win, d_wout)}
    narrowed = {}

    def upcoming(i):
        if i >= depth:
            return []
        mats = [(w, i // n_mixers) for w in mixer_weights.get(i % n_mixers, ())]
        return mats + [(mlp_w1, i)]

    def weight(w, layer):
        b = narrowed.get((id(w), layer))
        return (w, layer) if b is None else (b[None], 0)

    def record(sides, results):
        for (w, layer), b in zip(sides, results):
            narrowed[(id(w), layer)] = b

    for i in range(depth):
        kind, j = i % n_mixers, i // n_mixers
        if kind == 1:
            xf, xb = _mixer_pool_ln(xf, b_wgrp[j], b_scale[j], ln_g[i, 0], ln_b[i, 0], alpha, B, S, D)
        else:
            if kind == 0:
                sides = [s for s in [(mlp_w1, i), (a_wo, j)] if (id(s[0]), s[1]) not in narrowed]
                wo = None if sides and sides[-1][0] is a_wo else weight(a_wo, j)
                v, results = _mixer_dilated(xf, xb, weight(a_wqkv, j), wo, alpha, B, S, D, narrow=sides)
                record(sides, results)
            elif kind == 2:
                v = _mixer_fox(xf, xb, weight(c_win, j), c_bf[j], weight(c_wo, j), alpha, B, S, D)
            else:
                v = _mixer_conv(xf, xb, weight(d_win, j), d_conv[j], weight(d_wout, j), alpha, B, S, D)
            xf, xb = _layer_norm(v, ln_g[i, 0], ln_b[i, 0])
        sides = [(mlp_w2, i)] + upcoming(i + 1)
        hid, results = _matmul(xb, *weight(mlp_w1, i), n_cols=mlp_w1.shape[2], out_dtype=BF16, act="relu2", narrow=sides)
        record(sides, results)
        v = _matmul_bf16w(hid, *weight(mlp_w2, i), xf, alpha)
        xf, xb = _layer_norm(v, ln_g[i, 1], ln_b[i, 1])
    return xf.reshape(B, S, D).astype(x.dtype)
```

```python
import functools

import jax
import jax.numpy as jnp
from jax import lax
from jax.experimental import pallas as pl
from jax.experimental.pallas import tpu as pltpu

N_HEADS = 32
ROPE_THETA = 10000.0
DILATED_PATTERNS = ((128, 1), (512, 4), (2048, 16))
POOL_WINDOWS = (2, 4, 8, 16)
CONV_WIDTH = 3
LN_EPS = 1e-5
LANES = 128
MXU_WIDTH = 256
HALO_ROWS = 16
VMEM_LIMIT_BYTES = 56 * 1024 * 1024
VMEM_LIMIT_LARGE_BYTES = 60000 * 1024
MASK_VALUE = -1e30
LOG2_E = 1.4426950408889634

BF16 = jnp.bfloat16
F32 = jnp.float32


def _tile(n, pref):
    t = min(n, pref)
    while n % t:
        assert t % 2 == 0, (n, pref)
        t //= 2
    return t


def _params(*sem, vmem_limit_bytes=VMEM_LIMIT_BYTES):
    return pltpu.CompilerParams(dimension_semantics=sem, vmem_limit_bytes=vmem_limit_bytes)


def _mm_acc_body(x_ref, w_ref, r_ref, o_ref, *, alpha):
    k = pl.program_id(2)
    x = x_ref[...]
    width = min(o_ref.shape[1], MXU_WIDTH)
    for c in range(o_ref.shape[1] // width):
        cs = slice(c * width, (c + 1) * width)
        acc = jnp.dot(x, w_ref[:, cs], preferred_element_type=F32)
        o_ref[:, cs] = jnp.where(k == 0, alpha * r_ref[:, cs], o_ref[:, cs]) + acc


def _matmul_bf16w(x, w, layer, residual, alpha, *, bm=1024, bn=1024, bk=4096):
    M, K = x.shape
    N = w.shape[2]
    bm, bn, bk = _tile(M, bm), _tile(N, bn), _tile(K, bk)
    out_spec = pl.BlockSpec((bm, bn), lambda i, j, k: (i, j))
    return pl.pallas_call(
        functools.partial(_mm_acc_body, alpha=alpha),
        grid=(M // bm, N // bn, K // bk),
        in_specs=[
            pl.BlockSpec((bm, bk), lambda i, j, k: (i, k)),
            pl.BlockSpec((None, bk, bn), lambda i, j, k: (layer, k, j)),
            out_spec,
        ],
        out_specs=out_spec,
        out_shape=jax.ShapeDtypeStruct((M, N), F32),
        compiler_params=_params("parallel", "parallel", "arbitrary", vmem_limit_bytes=VMEM_LIMIT_LARGE_BYTES),
        name="matmul_ktiled",
    )(x, w, residual)


def _mm_body(*refs, act, has_scale, rope, alpha, n_side, narrow_w):
    x_ref, w_ref = refs[0], refs[1]
    n_extra = 2 if rope else (1 if has_scale else 0)
    n_in = 2 + n_extra + (alpha is not None) + n_side
    o_ref = refs[n_in]
    for t in range(n_side):
        refs[n_in + 1 + t][...] = refs[n_in - n_side + t][...].astype(BF16)

    if narrow_w:
        wb_ref = refs[-1]

        @pl.when(pl.program_id(1) == 0)
        def _():
            wb_ref[...] = w_ref[...].astype(BF16)

        w_ref = wb_ref

    acc = jnp.dot(x_ref[...], w_ref[...], preferred_element_type=F32)
    if rope:
        cos, sin = refs[2][0], refs[3][0]
        for h in range(acc.shape[1] // LANES):
            cs = slice(h * LANES, (h + 1) * LANES)
            a = acc[:, cs]
            o_ref[:, cs] = (a * cos + pltpu.roll(a, LANES // 2, 1) * sin).astype(o_ref.dtype)
        return
    if act == "relu2":
        acc = jnp.maximum(acc, 0.0)
        acc = acc * acc
    if has_scale:
        acc = acc * refs[2][...]
    if alpha is not None:
        acc = alpha * refs[2 + n_extra][...] + acc
    o_ref[...] = acc.astype(o_ref.dtype)


def _matmul(x, w, layer, *, n_cols, out_dtype, col_offset=0, act=None, col_scale=None, rope=None, residual=None,
            alpha=None, narrow=(), bm=1024, bn=None):
    M, K = x.shape
    N = n_cols
    narrow_w = w.dtype != BF16
    if bn is None:
        bn = 512 if narrow_w else 1024
    if rope is not None:
        cos_tab, sin_tab, seq = rope
        bm, bn = _tile(seq, bm), _tile(N // 2, bn)
    else:
        bm, bn = _tile(M, bm), _tile(N, bn)
    assert col_offset % bn == 0 and (residual is None) == (alpha is None)
    off = col_offset // bn
    n_i = M // bm
    n_steps = (N // bn) * n_i
    out_spec = pl.BlockSpec((bm, bn), lambda j, i: (i, j))
    in_specs = [
        pl.BlockSpec((bm, K), lambda j, i: (i, 0)),
        pl.BlockSpec((None, K, bn), lambda j, i: (layer, 0, j + off)),
    ]
    args = [x, w]
    if rope is not None:
        tiles_per_part, seq_tiles = (N // 2) // bn, seq // bm
        tab_spec = pl.BlockSpec((1, bm, LANES), lambda j, i: (j // tiles_per_part, i % seq_tiles, 0))
        in_specs += [tab_spec, tab_spec]
        args += [cos_tab, sin_tab]
    elif col_scale is not None:
        in_specs.append(pl.BlockSpec((1, bn), lambda j, i: (0, j)))
        args.append(col_scale.reshape(1, N).astype(F32))
    if residual is not None:
        in_specs.append(out_spec)
        args.append(residual)
    out_specs, out_shape = [out_spec], [jax.ShapeDtypeStruct((M, N), out_dtype)]
    for src, src_layer in narrow:
        rows, width = src.shape[1], src.shape[2]
        slab = rows // n_steps
        assert slab * n_steps == rows and slab % HALO_ROWS == 0
        in_specs.append(pl.BlockSpec((None, slab, width), lambda j, i, src_layer=src_layer: (src_layer, j * n_i + i, 0)))
        args.append(src)
        out_specs.append(pl.BlockSpec((slab, width), lambda j, i: (j * n_i + i, 0)))
        out_shape.append(jax.ShapeDtypeStruct((rows, width), BF16))
    outs = pl.pallas_call(
        functools.partial(
            _mm_body, act=act, has_scale=col_scale is not None, rope=rope is not None, alpha=alpha,
            n_side=len(narrow), narrow_w=narrow_w,
        ),
        grid=(N // bn, n_i),
        in_specs=in_specs,
        out_specs=out_specs,
        out_shape=out_shape,
        scratch_shapes=[pltpu.VMEM((K, bn), BF16)] if narrow_w else [],
        compiler_params=_params("arbitrary", "arbitrary"),
        name="matmul",
    )(*args)
    return (outs[0], outs[1:]) if narrow else outs[0]


def _layer_norm_rows(v, g, b):
    mu = jnp.mean(v, axis=-1, keepdims=True)
    xc = v - mu
    var = jnp.mean(xc * xc, axis=-1, keepdims=True)
    return xc * lax.rsqrt(var + LN_EPS) * g + b


def _ln_body(v_ref, g_ref, b_ref, of_ref, ob_ref):
    out = _layer_norm_rows(v_ref[...], g_ref[...], b_ref[...])
    of_ref[...] = out
    ob_ref[...] = out.astype(BF16)


def _layer_norm(v, g, b, *, tm=256):
    M, D = v.shape
    tm = _tile(M, tm)
    row = pl.BlockSpec((tm, D), lambda i: (i, 0))
    vec = pl.BlockSpec((1, D), lambda i: (0, 0))
    return pl.pallas_call(
        _ln_body,
        grid=(M // tm,),
        in_specs=[row, vec, vec],
        out_specs=[row, row],
        out_shape=[jax.ShapeDtypeStruct((M, D), F32), jax.ShapeDtypeStruct((M, D), BF16)],
        compiler_params=_params("parallel"),
        name="layer_norm",
    )(v, g.reshape(1, D), b.reshape(1, D))


def _dilated_fused_body(q_ref, k_ref, v_ref, o_ref, *scratch, patterns, unroll, heads):
    for t in range(heads):
        _dilated_head(q_ref, k_ref, v_ref, o_ref, *scratch, cols=slice(t * LANES, (t + 1) * LANES), patterns=patterns,
                      unroll=unroll)


def _dilated_head(q_ref, k_ref, v_ref, o_ref, q32, k32, v32, qc, kc, vc, ob, lb, *, cols, patterns, unroll):
    S = q_ref.shape[1]
    q32[...] = q_ref[0, :, cols].astype(F32)
    k32[...] = k_ref[0, :, cols].astype(F32)
    v32[...] = v_ref[0, :, cols].astype(F32)
    nt = (((1,), (1,)), ((), ()))
    for g, (window, d) in enumerate(patterns):
        blk = window // d
        sc = S // d
        n_blocks = sc // blk
        per_iter = min(n_blocks, unroll)
        classes = min(d, max(1, unroll // n_blocks))
        pad = sc + blk
        for c in range(classes):
            kc[c * pad : c * pad + blk, :] = jnp.zeros((blk, LANES), BF16)
            vc[c * pad : c * pad + blk, :] = jnp.zeros((blk, LANES), BF16)
        qi = lax.broadcasted_iota(jnp.int32, (blk, 2 * blk), 0)
        kj = lax.broadcasted_iota(jnp.int32, (blk, 2 * blk), 1)

        def class_body(rr, carry, g=g, d=d, blk=blk, sc=sc, n_blocks=n_blocks, per_iter=per_iter, classes=classes,
                       pad=pad, qi=qi, kj=kj):
            for c in range(classes):
                if d == 1:
                    qc[0:sc, :] = q_ref[0, :, cols]
                    kc[blk:pad, :] = k_ref[0, :, cols]
                    vc[blk:pad, :] = v_ref[0, :, cols]
                else:
                    rows = pl.ds(rr * classes + c, sc, stride=d)
                    qc[c * sc : (c + 1) * sc, :] = q32[rows, :].astype(BF16)
                    kc[c * pad + blk : (c + 1) * pad, :] = k32[rows, :].astype(BF16)
                    vc[c * pad + blk : (c + 1) * pad, :] = v32[rows, :].astype(BF16)

            def blocks(it, carry2):
                units = [
                    (c, pl.multiple_of((it * per_iter + u) * blk, blk)) for c in range(classes) for u in range(per_iter)
                ]
                scores = [
                    lax.dot_general(
                        qc[pl.ds(c * sc + n0, blk), :],
                        kc[pl.ds(c * pad + n0, 2 * blk), :],
                        nt,
                        preferred_element_type=F32,
                    )
                    for c, n0 in units
                ]
                probs, dens, lses = [], [], []
                for (c, n0), s in zip(units, scores):
                    visible = (kj >= jnp.maximum(qi, blk - n0)) & (kj <= qi + blk)
                    s = jnp.where(visible, s, MASK_VALUE)
                    m = jnp.max(s, axis=1, keepdims=True)
                    p = jnp.exp(s - m)
                    den = jnp.sum(p, axis=1, keepdims=True)
                    probs.append(p.astype(BF16))
                    dens.append(den)
                    lses.append(m + jnp.log(den))
                for (c, n0), p, den, lse in zip(units, probs, dens, lses):
                    o = jnp.dot(p, vc[pl.ds(c * pad + n0, 2 * blk), :], preferred_element_type=F32) / den
                    out_rows = pl.ds(rr * classes + c + n0 * d, blk, stride=d) if d > 1 else pl.ds(n0, blk)
                    ob[g, out_rows, :] = o
                    lb[g, out_rows, :] = jnp.broadcast_to(lse, (blk, LANES))
                return carry2

            lax.fori_loop(0, n_blocks // per_iter, blocks, 0)
            return carry

        lax.fori_loop(0, d // classes, class_body, 0)

    merge_rows = min(S, 512)

    def merge(t, carry):
        rows = pl.ds(pl.multiple_of(t * merge_rows, merge_rows), merge_rows)
        lses = [lb[g, rows, :] for g in range(len(patterns))]
        m = functools.reduce(jnp.maximum, lses)
        es = [jnp.exp(l - m) for l in lses]
        tot = functools.reduce(lambda a, b: a + b, es)
        acc = functools.reduce(lambda a, b: a + b, [e * ob[g, rows, :] for g, e in enumerate(es)])
        o_ref[0, rows, cols] = (acc / tot).astype(o_ref.dtype)
        return carry

    lax.fori_loop(0, S // merge_rows, merge, 0)


def _dilated_attention(qk, v, B, S, D, patterns, *, unroll=4, heads=2):
    W = heads * LANES
    groups = D // W
    for window, d in patterns:
        assert S % window == 0
    max_blk = max(w // d for w, d in patterns)
    part = lambda p: pl.BlockSpec((1, S, W), lambda b, g: (b, 0, p * groups + g))
    return pl.pallas_call(
        functools.partial(_dilated_fused_body, patterns=patterns, unroll=unroll, heads=heads),
        grid=(B, groups),
        in_specs=[part(0), part(1), part(0)],
        out_specs=part(0),
        out_shape=jax.ShapeDtypeStruct((B, S, D), BF16),
        scratch_shapes=[
            pltpu.VMEM((S, LANES), F32),
            pltpu.VMEM((S, LANES), F32),
            pltpu.VMEM((S, LANES), F32),
            pltpu.VMEM((S, LANES), BF16),
            pltpu.VMEM((S + unroll * max_blk, LANES), BF16),
            pltpu.VMEM((S + unroll * max_blk, LANES), BF16),
            pltpu.VMEM((len(patterns), S, LANES), F32),
            pltpu.VMEM((len(patterns), S, LANES), F32),
        ],
        compiler_params=_params("parallel", "parallel"),
        name="dilated_attention",
    )(qk.reshape(B, S, 2 * D), qk.reshape(B, S, 2 * D), v.reshape(B, S, D))


def _rope_tables(S, hd, q_scale):
    pos = jnp.arange(S, dtype=F32)
    inv = ROPE_THETA ** (-jnp.arange(0, hd, 2, dtype=F32) / hd)
    ang = pos[:, None] * inv[None, :]
    ang = jnp.concatenate([ang, ang], axis=-1)
    sign = jnp.concatenate([-jnp.ones((hd // 2,), F32), jnp.ones((hd // 2,), F32)])
    cos, sin = jnp.cos(ang), jnp.sin(ang) * sign
    return jnp.stack([cos * q_scale, cos]), jnp.stack([sin * q_scale, sin])


def _mixer_dilated(xf, xb, wqkv, wo, alpha, B, S, D, narrow=()):
    hd = D // N_HEADS
    assert hd == LANES
    cos_tab, sin_tab = _rope_tables(S, hd, hd**-0.5)
    qk = _matmul(xb, *wqkv, n_cols=2 * D, out_dtype=BF16, rope=(cos_tab, sin_tab, S), narrow=narrow)
    narrowed = ()
    if narrow:
        qk, narrowed = qk
    if wo is None:
        wo = (narrowed[-1][None], 0)
    v = _matmul(xb, *wqkv, n_cols=D, col_offset=2 * D, out_dtype=BF16)
    o = _dilated_attention(qk, v, B, S, D, DILATED_PATTERNS)
    out = _matmul(o.reshape(B * S, D), *wo, n_cols=D, out_dtype=F32, residual=xf, alpha=alpha)
    return out, narrowed


def _pool_ln_body(x_ref, xh_ref, w_ref, sc_ref, g_ref, b_ref, of_ref, ob_ref, xs_ref, y_ref, *, alpha, ts):
    i = pl.program_id(1)
    x = x_ref[0]
    xs_ref[0:HALO_ROWS, :] = jnp.where(i > 0, xh_ref[0], 0.0)
    xs_ref[HALO_ROWS:, :] = x
    cg = x.shape[1] // len(POOL_WINDOWS)
    t1 = (i * ts + 1 + lax.broadcasted_iota(jnp.int32, (ts, 1), 0)).astype(F32)
    for g, win in enumerate(POOL_WINDOWS):
        cs = slice(g * cg, (g + 1) * cg)
        tot = x[:, cs]
        for j in range(1, win):
            tot = tot + xs_ref[pl.ds(HALO_ROWS - j, ts), cs]
        pooled = tot / jnp.minimum(t1, float(win)) - x[:, cs]
        y = jnp.dot(pooled.astype(BF16), w_ref[g], preferred_element_type=F32)
        y_ref[:, cs] = y * sc_ref[:, cs]
    out = _layer_norm_rows(alpha * x + y_ref[...], g_ref[...], b_ref[...])
    of_ref[0] = out
    ob_ref[0] = out.astype(BF16)


def _mixer_pool_ln(xf, wgrp, scale, g, b, alpha, B, S, D, *, ts=256):
    ts = _tile(S, ts)
    assert ts % HALO_ROWS == 0 and max(POOL_WINDOWS) <= HALO_ROWS
    G, cg, _ = wgrp.shape
    x3 = xf.reshape(B, S, D)
    row = pl.BlockSpec((1, ts, D), lambda bb, i: (bb, i, 0))
    halo = pl.BlockSpec((1, HALO_ROWS, D), lambda bb, i: (bb, jnp.maximum(i * (ts // HALO_ROWS) - 1, 0), 0))
    vec = pl.BlockSpec((1, D), lambda bb, i: (0, 0))
    of, ob = pl.pallas_call(
        functools.partial(_pool_ln_body, alpha=alpha, ts=ts),
        grid=(B, S // ts),
        in_specs=[row, halo, pl.BlockSpec((G, cg, cg), lambda bb, i: (0, 0, 0)), vec, vec, vec],
        out_specs=[row, row],
        out_shape=[jax.ShapeDtypeStruct((B, S, D), F32), jax.ShapeDtypeStruct((B, S, D), BF16)],
        scratch_shapes=[pltpu.VMEM((ts + HALO_ROWS, D), F32), pltpu.VMEM((ts, D), F32)],
        compiler_params=_params("parallel", "parallel"),
        name="pool_ln",
    )(x3, x3, wgrp.astype(BF16), scale.reshape(1, D), g.reshape(1, D), b.reshape(1, D))
    return of.reshape(B * S, D), ob.reshape(B * S, D)


def _fgate_body(z_ref, bf_ref, c_ref):
    z = z_ref[0] + bf_ref[...]
    c = jnp.minimum(z, 0.0) - jnp.log1p(jnp.exp(-jnp.abs(z)))
    S = c.shape[0]
    row = lax.broadcasted_iota(jnp.int32, c.shape, 0)
    shift = 1
    while shift < S:
        c = c + jnp.where(row >= shift, pltpu.roll(c, shift, 0), 0.0)
        shift *= 2
    c_ref[0] = c


def _forget_cumsum(z, bias_row):
    B, S, W = z.shape
    blk = pl.BlockSpec((1, S, W), lambda b: (b, 0, 0))
    return pl.pallas_call(
        _fgate_body,
        grid=(B,),
        in_specs=[blk, pl.BlockSpec((1, W), lambda b: (0, 0))],
        out_specs=blk,
        out_shape=jax.ShapeDtypeStruct((B, S, W), F32),
        compiler_params=_params("parallel"),
        name="forget_cumsum",
    )(z, bias_row)


def _fox_body(q_ref, k_ref, v_ref, cc_ref, o_ref, qa_ref, ka_ref, *, bq, bk, heads):
    S = q_ref.shape[1]
    diag_steps = bq // bk
    lane = lax.broadcasted_iota(jnp.int32, (S, LANES), 1)
    for t in range(heads):
        head_lane = pl.program_id(1) * heads + t
        c = jnp.sum(jnp.where(lane == head_lane, cc_ref[0], 0.0), axis=1, keepdims=True) * LOG2_E
        hi = c.astype(BF16).astype(F32)
        mid = (c - hi).astype(BF16).astype(F32)
        lo = c - hi - mid

        def pieces(sign, first, hi=hi, mid=mid, lo=lo):
            return (
                jnp.where(lane == first, sign * hi, 0.0)
                + jnp.where(lane == first + 1, sign * mid, 0.0)
                + jnp.where(lane == first + 2, sign * lo, 0.0)
            )

        qa_ref[t] = (pieces(1.0, 0) + jnp.where((lane >= 3) & (lane < 6), 1.0, 0.0)).astype(BF16)
        ka_ref[t] = (jnp.where(lane < 3, 1.0, 0.0) + pieces(-1.0, 3)).astype(BF16)

    key_minus_query = lax.broadcasted_iota(jnp.int32, (bq, bk), 1) - lax.broadcasted_iota(jnp.int32, (bq, bk), 0)
    nt = (((1,), (1,)), ((), ()))
    across = lambda t: jnp.concatenate([t] * (bk // LANES), axis=1)
    cols = [slice(t * LANES, (t + 1) * LANES) for t in range(heads)]

    def q_block(i, carry):
        r0 = pl.multiple_of(i * bq, bq)
        rows = pl.ds(r0, bq)
        qs = [jnp.concatenate([q_ref[0, rows, cs], qa_ref[t, rows, :]], axis=1) for t, cs in enumerate(cols)]

        def k_step(c0, states, diag_offset):
            keys = pl.ds(c0, bk)
            scores = [
                lax.dot_general(
                    q, jnp.concatenate([k_ref[0, keys, cs], ka_ref[t, keys, :]], axis=1), nt, preferred_element_type=F32
                )
                for t, (q, cs) in enumerate(zip(qs, cols))
            ]
            new_states = []
            for s, (m, l, acc), cs in zip(scores, states, cols):
                if diag_offset is not None:
                    s = jnp.where(key_minus_query <= -diag_offset, s, MASK_VALUE)
                m_new = jnp.maximum(m, jnp.max(s, axis=1, keepdims=True))
                a = jnp.exp2(m - m_new)
                p = jnp.exp2(s - across(m_new))
                l = a * l + jnp.sum(p, axis=1, keepdims=True)
                acc = a * acc + jnp.dot(p.astype(BF16), v_ref[0, keys, cs], preferred_element_type=F32)
                new_states.append((m_new, l, acc))
            return tuple(new_states)

        init = (jnp.full((bq, LANES), MASK_VALUE, F32), jnp.zeros((bq, LANES), F32), jnp.zeros((bq, LANES), F32))
        states = lax.fori_loop(
            0, i * diag_steps, lambda j, st: k_step(pl.multiple_of(j * bk, bk), st, None), (init,) * heads
        )
        for jj in range(diag_steps):
            states = k_step(pl.multiple_of(r0 + jj * bk, bk), states, jj * bk)
        for (_, l, acc), cs in zip(states, cols):
            o_ref[0, rows, cs] = (acc / l).astype(o_ref.dtype)
        return carry

    lax.fori_loop(0, S // bq, q_block, 0)


def _fox_attention(proj, c_col, B, S, D, *, bq=1024, bk=1024, heads=2):
    bq = _tile(S, bq)
    bk = _tile(bq, bk)
    W = heads * LANES
    groups = D // W
    part = lambda p: pl.BlockSpec((1, S, W), lambda b, g: (b, 0, p * groups + g))
    return pl.pallas_call(
        functools.partial(_fox_body, bq=bq, bk=bk, heads=heads),
        grid=(B, groups),
        in_specs=[part(0), part(1), part(2), pl.BlockSpec((1, S, LANES), lambda b, g: (b, 0, 0))],
        out_specs=pl.BlockSpec((1, S, W), lambda b, g: (b, 0, g)),
        out_shape=jax.ShapeDtypeStruct((B, S, D), BF16),
        scratch_shapes=[pltpu.VMEM((heads, S, LANES), BF16), pltpu.VMEM((heads, S, LANES), BF16)],
        compiler_params=_params("parallel", "arbitrary"),
        name="fox_attention",
    )(proj, proj, proj, c_col)


def _mixer_fox(xf, xb, win, bf, wo, alpha, B, S, D):
    H = N_HEADS
    assert D // H == LANES and H <= LANES
    col_scale = jnp.concatenate([jnp.full((D,), LANES**-0.5 * LOG2_E, F32), jnp.ones((2 * D,), F32)])
    proj = _matmul(xb, *win, n_cols=3 * D, out_dtype=BF16, col_scale=col_scale)
    w_gate = jnp.pad(win[0][win[1], :, 3 * D :], ((0, 0), (0, LANES - H)))[None]
    z = _matmul(xb, w_gate, 0, n_cols=LANES, out_dtype=F32)
    bias_row = jnp.pad(bf.astype(F32), (0, LANES - H)).reshape(1, LANES)
    c_col = _forget_cumsum(z.reshape(B, S, LANES), bias_row)
    o = _fox_attention(proj.reshape(B, S, 3 * D), c_col, B, S, D)
    return _matmul(o.reshape(B * S, D), *wo, n_cols=D, out_dtype=F32, residual=xf, alpha=alpha)


def _convgate_body(b_ref, c_ref, h_ref, ch_ref, hh_ref, w_ref, o_ref, zs_ref, *, ts):
    i = pl.program_id(1)
    z = c_ref[0].astype(F32) * h_ref[0].astype(F32)
    zh = ch_ref[0].astype(F32) * hh_ref[0].astype(F32)
    zs_ref[0:HALO_ROWS, :] = jnp.where(i > 0, zh, 0.0)
    zs_ref[HALO_ROWS:, :] = z
    w = w_ref[...]
    conv = w[CONV_WIDTH - 1 : CONV_WIDTH] * z
    for j in range(1, CONV_WIDTH):
        tap = CONV_WIDTH - 1 - j
        conv = conv + w[tap : tap + 1] * zs_ref[pl.ds(HALO_ROWS - j, ts), :]
    o_ref[0] = (b_ref[0].astype(F32) * conv).astype(o_ref.dtype)


def _conv_gate(proj, wconv, B, S, D, *, ts=512, tc=1024):
    ts, tc = _tile(S, ts), _tile(D, tc)
    assert ts % HALO_ROWS == 0 and CONV_WIDTH - 1 <= HALO_ROWS
    nct = D // tc
    cur = lambda part: pl.BlockSpec((1, ts, tc), lambda b, i, j: (b, i, part * nct + j))
    halo = lambda part: pl.BlockSpec(
        (1, HALO_ROWS, tc), lambda b, i, j: (b, jnp.maximum(i * (ts // HALO_ROWS) - 1, 0), part * nct + j)
    )
    return pl.pallas_call(
        functools.partial(_convgate_body, ts=ts),
        grid=(B, S // ts, nct),
        in_specs=[cur(0), cur(1), cur(2), halo(1), halo(2), pl.BlockSpec((CONV_WIDTH, tc), lambda b, i, j: (0, j))],
        out_specs=pl.BlockSpec((1, ts, tc), lambda b, i, j: (b, i, j)),
        out_shape=jax.ShapeDtypeStruct((B, S, D), BF16),
        scratch_shapes=[pltpu.VMEM((ts + HALO_ROWS, tc), F32)],
        compiler_params=_params("parallel", "parallel", "parallel"),
        name="conv_gate",
    )(proj, proj, proj, proj, proj, wconv.astype(F32))


def _mixer_conv(xf, xb, win, wconv, wout, alpha, B, S, D):
    proj = _matmul(xb, *win, n_cols=3 * D, out_dtype=BF16)
    u = _conv_gate(proj.reshape(B, S, 3 * D), wconv, B, S, D)
    return _matmul(u.reshape(B * S, D), *wout, n_cols=D, out_dtype=F32, residual=xf, alpha=alpha)


def kernel(x, ln_g, ln_b, mlp_w1, mlp_w2, a_wqkv, a_wo, b_wgrp, b_scale, c_win, c_bf, c_wo, d_win, d_conv, d_wout):
    B, S, D = x.shape
    depth = ln_g.shape[0]
    n_mixers = 4
    alpha = (2.0 * depth) ** 0.25
    xf = x.reshape(B * S, D).astype(F32)
    xb = xf.astype(BF16)

    mixer_weights = {0: (a_wqkv, a_wo), 2: (c_win, c_wo), 3: (d_win, d_wout)}
    narrowed = {}

    def upcoming(i):
        if i >= depth:
            return []
        mats = [(w, i // n_mixers) for w in mixer_weights.get(i % n_mixers, ())]
        return mats + [(mlp_w1, i)]

    def weight(w, layer):
        b = narrowed.get((id(w), layer))
        return (w, layer) if b is None else (b[None], 0)

    def record(sides, results):
        for (w, layer), b in zip(sides, results):
            narrowed[(id(w), layer)] = b

    for i in range(depth):
        kind, j = i % n_mixers, i // n_mixers
        if kind == 1:
            xf, xb = _mixer_pool_ln(xf, b_wgrp[j], b_scale[j], ln_g[i, 0], ln_b[i, 0], alpha, B, S, D)
        else:
            if kind == 0:
                sides = [s for s in [(mlp_w1, i), (a_wo, j)] if (id(s[0]), s[1]) not in narrowed]
                wo = None if sides and sides[-1][0] is a_wo else weight(a_wo, j)
                v, results = _mixer_dilated(xf, xb, weight(a_wqkv, j), wo, alpha, B, S, D, narrow=sides)
                record(sides, results)
            elif kind == 2:
                v = _mixer_fox(xf, xb, weight(c_win, j), c_bf[j], weight(c_wo, j), alpha, B, S, D)
            else:
                v = _mixer_conv(xf, xb, weight(d_win, j), d_conv[j], weight(d_wout, j), alpha, B, S, D)
            xf, xb = _layer_norm(v, ln_g[i, 0], ln_b[i, 0])
        sides = [(mlp_w2, i)] + upcoming(i + 1)
        hid, results = _matmul(xb, *weight(mlp_w1, i), n_cols=mlp_w1.shape[2], out_dtype=BF16, act="relu2", narrow=sides)
        record(sides, results)
        v = _matmul_bf16w(hid, *weight(mlp_w2, i), xf, alpha)
        xf, xb = _layer_norm(v, ln_g[i, 1], ln_b[i, 1])
    return xf.reshape(B, S, D).astype(x.dtype)
```

```python
import functools

import jax
import jax.numpy as jnp
from jax import lax
from jax.experimental import pallas as pl
from jax.experimental.pallas import tpu as pltpu

N_HEADS = 32
ROPE_THETA = 10000.0
DILATED_PATTERNS = ((128, 1), (512, 4), (2048, 16))
POOL_WINDOWS = (2, 4, 8, 16)
CONV_WIDTH = 3
LN_EPS = 1e-5
LANES = 128
MXU_WIDTH = 256
HALO_ROWS = 16
VMEM_LIMIT_BYTES = 56 * 1024 * 1024
VMEM_LIMIT_LARGE_BYTES = 60000 * 1024
MASK_VALUE = -1e30
LOG2_E = 1.4426950408889634

BF16 = jnp.bfloat16
F32 = jnp.float32


def _tile(n, pref):
    t = min(n, pref)
    while n % t:
        assert t % 2 == 0, (n, pref)
        t //= 2
    return t


def _params(*sem, vmem_limit_bytes=VMEM_LIMIT_BYTES):
    return pltpu.CompilerParams(dimension_semantics=sem, vmem_limit_bytes=vmem_limit_bytes)


def _mm_acc_body(x_ref, w_ref, r_ref, o_ref, *, alpha):
    k = pl.program_id(2)
    x = x_ref[...]
    width = min(o_ref.shape[1], MXU_WIDTH)
    for c in range(o_ref.shape[1] // width):
        cs = slice(c * width, (c + 1) * width)
        acc = jnp.dot(x, w_ref[:, cs], preferred_element_type=F32)
        o_ref[:, cs] = jnp.where(k == 0, alpha * r_ref[:, cs], o_ref[:, cs]) + acc


def _matmul_bf16w(x, w, layer, residual, alpha, *, bm=1024, bn=1024, bk=4096):
    M, K = x.shape
    N = w.shape[2]
    bm, bn, bk = _tile(M, bm), _tile(N, bn), _tile(K, bk)
    out_spec = pl.BlockSpec((bm, bn), lambda i, j, k: (i, j))
    return pl.pallas_call(
        functools.partial(_mm_acc_body, alpha=alpha),
        grid=(M // bm, N // bn, K // bk),
        in_specs=[
            pl.BlockSpec((bm, bk), lambda i, j, k: (i, k)),
            pl.BlockSpec((None, bk, bn), lambda i, j, k: (layer, k, j)),
            out_spec,
        ],
        out_specs=out_spec,
        out_shape=jax.ShapeDtypeStruct((M, N), F32),
        compiler_params=_params("parallel", "parallel", "arbitrary", vmem_limit_bytes=VMEM_LIMIT_LARGE_BYTES),
        name="matmul_ktiled",
    )(x, w, residual)


def _mm_body(*refs, act, has_scale, rope, alpha, n_side, narrow_w, head_major):
    x_ref, w_ref = refs[0], refs[1]
    n_extra = 2 if rope else (1 if has_scale else 0)
    n_in = 2 + n_extra + (alpha is not None) + n_side
    o_ref = refs[n_in]
    for t in range(n_side):
        refs[n_in + 1 + t][...] = refs[n_in - n_side + t][...].astype(BF16)

    if narrow_w:
        wb_ref = refs[-1]

        @pl.when(pl.program_id(1) == 0)
        def _():
            wb_ref[...] = w_ref[...].astype(BF16)

        w_ref = wb_ref

    x = x_ref[...]
    n_out = w_ref.shape[1]
    width = min(n_out, MXU_WIDTH)
    for c in range(n_out // width):
        cs = slice(c * width, (c + 1) * width)
        acc = jnp.dot(x, w_ref[:, cs], preferred_element_type=F32)
        if rope:
            cos, sin = refs[2][0], refs[3][0]
            for h in range(width // LANES):
                a = acc[:, h * LANES : (h + 1) * LANES]
                hs = slice(c * width + h * LANES, c * width + (h + 1) * LANES)
                roped = (a * cos + pltpu.roll(a, LANES // 2, 1) * sin).astype(o_ref.dtype)
                if head_major:
                    o_ref[c * (width // LANES) + h] = roped
                else:
                    o_ref[:, hs] = roped
            continue
        if act == "relu2":
            acc = jnp.maximum(acc, 0.0)
            acc = acc * acc
        if has_scale:
            acc = acc * refs[2][:, cs]
        if alpha is not None:
            acc = alpha * refs[2 + n_extra][:, cs] + acc
        if head_major:
            for h in range(width // LANES):
                o_ref[c * (width // LANES) + h] = acc[:, h * LANES : (h + 1) * LANES].astype(o_ref.dtype)
        else:
            o_ref[:, cs] = acc.astype(o_ref.dtype)


def _matmul(x, w, layer, *, n_cols, out_dtype, col_offset=0, act=None, col_scale=None, rope=None, residual=None,
            alpha=None, narrow=(), head_major=False, bm=1024, bn=None):
    M, K = x.shape
    N = n_cols
    narrow_w = w.dtype != BF16
    if bn is None:
        bn = 512 if narrow_w else 1024
    if rope is not None:
        cos_tab, sin_tab, seq = rope
        bm, bn = _tile(seq, bm), _tile(N // 2, bn)
    else:
        bm, bn = _tile(M, bm), _tile(N, bn)
    assert col_offset % bn == 0 and (residual is None) == (alpha is None)
    off = col_offset // bn
    n_i = M // bm
    n_steps = (N // bn) * n_i
    out_spec = pl.BlockSpec((bm, bn), lambda j, i: (i, j))
    in_specs = [
        pl.BlockSpec((bm, K), lambda j, i: (i, 0)),
        pl.BlockSpec((None, K, bn), lambda j, i: (layer, 0, j + off)),
    ]
    args = [x, w]
    if rope is not None:
        tiles_per_part, seq_tiles = (N // 2) // bn, seq // bm
        tab_spec = pl.BlockSpec((1, bm, LANES), lambda j, i: (j // tiles_per_part, i % seq_tiles, 0))
        in_specs += [tab_spec, tab_spec]
        args += [cos_tab, sin_tab]
    elif col_scale is not None:
        in_specs.append(pl.BlockSpec((1, bn), lambda j, i: (0, j)))
        args.append(col_scale.reshape(1, N).astype(F32))
    if residual is not None:
        in_specs.append(out_spec)
        args.append(residual)
    out_specs, out_shape = [out_spec], [jax.ShapeDtypeStruct((M, N), out_dtype)]
    if head_major:
        assert residual is None and bn % LANES == 0
        out_specs = [pl.BlockSpec((bn // LANES, bm, LANES), lambda j, i: (j, i, 0))]
        out_shape = [jax.ShapeDtypeStruct((N // LANES, M, LANES), out_dtype)]
    for src, src_layer in narrow:
        rows, width = src.shape[1], src.shape[2]
        slab = rows // n_steps
        assert slab * n_steps == rows and slab % HALO_ROWS == 0
        in_specs.append(pl.BlockSpec((None, slab, width), lambda j, i, src_layer=src_layer: (src_layer, j * n_i + i, 0)))
        args.append(src)
        out_specs.append(pl.BlockSpec((slab, width), lambda j, i: (j * n_i + i, 0)))
        out_shape.append(jax.ShapeDtypeStruct((rows, width), BF16))
    outs = pl.pallas_call(
        functools.partial(
            _mm_body, act=act, has_scale=col_scale is not None, rope=rope is not None, alpha=alpha,
            n_side=len(narrow), narrow_w=narrow_w, head_major=head_major,
        ),
        grid=(N // bn, n_i),
        in_specs=in_specs,
        out_specs=out_specs,
        out_shape=out_shape,
        scratch_shapes=[pltpu.VMEM((K, bn), BF16)] if narrow_w else [],
        compiler_params=_params("arbitrary", "arbitrary"),
        name="matmul",
    )(*args)
    return (outs[0], outs[1:]) if narrow else outs[0]


def _layer_norm_rows(v, g, b):
    mu = jnp.mean(v, axis=-1, keepdims=True)
    xc = v - mu
    var = jnp.mean(xc * xc, axis=-1, keepdims=True)
    return xc * lax.rsqrt(var + LN_EPS) * g + b


def _ln_body(v_ref, g_ref, b_ref, of_ref, ob_ref):
    out = _layer_norm_rows(v_ref[...], g_ref[...], b_ref[...])
    of_ref[...] = out
    ob_ref[...] = out.astype(BF16)


def _layer_norm(v, g, b, *, tm=256):
    M, D = v.shape
    tm = _tile(M, tm)
    row = pl.BlockSpec((tm, D), lambda i: (i, 0))
    vec = pl.BlockSpec((1, D), lambda i: (0, 0))
    return pl.pallas_call(
        _ln_body,
        grid=(M // tm,),
        in_specs=[row, vec, vec],
        out_specs=[row, row],
        out_shape=[jax.ShapeDtypeStruct((M, D), F32), jax.ShapeDtypeStruct((M, D), BF16)],
        compiler_params=_params("parallel"),
        name="layer_norm",
    )(v, g.reshape(1, D), b.reshape(1, D))


def _dilated_fused_body(q_ref, k_ref, v_ref, o_ref, *scratch, patterns, unroll, heads):
    for t in range(heads):
        _dilated_head(q_ref, k_ref, v_ref, o_ref, *scratch, cols=slice(t * LANES, (t + 1) * LANES), patterns=patterns,
                      unroll=unroll)


def _dilated_head(q_ref, k_ref, v_ref, o_ref, q32, k32, v32, qc, kc, vc, ob, lb, *, cols, patterns, unroll):
    S = q_ref.shape[1]
    q32[...] = q_ref[0, :, cols].astype(F32)
    k32[...] = k_ref[0, :, cols].astype(F32)
    v32[...] = v_ref[0, :, cols].astype(F32)
    nt = (((1,), (1,)), ((), ()))
    for g, (window, d) in enumerate(patterns):
        blk = window // d
        sc = S // d
        n_blocks = sc // blk
        per_iter = min(n_blocks, unroll)
        classes = min(d, max(1, unroll // n_blocks))
        pad = sc + blk
        for c in range(classes):
            kc[c * pad : c * pad + blk, :] = jnp.zeros((blk, LANES), BF16)
            vc[c * pad : c * pad + blk, :] = jnp.zeros((blk, LANES), BF16)
        qi = lax.broadcasted_iota(jnp.int32, (blk, 2 * blk), 0)
        kj = lax.broadcasted_iota(jnp.int32, (blk, 2 * blk), 1)

        def class_body(rr, carry, g=g, d=d, blk=blk, sc=sc, n_blocks=n_blocks, per_iter=per_iter, classes=classes,
                       pad=pad, qi=qi, kj=kj):
            for c in range(classes):
                if d == 1:
                    qc[0:sc, :] = q_ref[0, :, cols]
                    kc[blk:pad, :] = k_ref[0, :, cols]
                    vc[blk:pad, :] = v_ref[0, :, cols]
                else:
                    rows = pl.ds(rr * classes + c, sc, stride=d)
                    qc[c * sc : (c + 1) * sc, :] = q32[rows, :].astype(BF16)
                    kc[c * pad + blk : (c + 1) * pad, :] = k32[rows, :].astype(BF16)
                    vc[c * pad + blk : (c + 1) * pad, :] = v32[rows, :].astype(BF16)

            def blocks(it, carry2):
                units = [
                    (c, pl.multiple_of((it * per_iter + u) * blk, blk)) for c in range(classes) for u in range(per_iter)
                ]
                scores = [
                    lax.dot_general(
                        qc[pl.ds(c * sc + n0, blk), :],
                        kc[pl.ds(c * pad + n0, 2 * blk), :],
                        nt,
                        preferred_element_type=F32,
                    )
                    for c, n0 in units
                ]
                probs, dens, lses = [], [], []
                for (c, n0), s in zip(units, scores):
                    visible = (kj >= jnp.maximum(qi, blk - n0)) & (kj <= qi + blk)
                    s = jnp.where(visible, s, MASK_VALUE)
                    m = jnp.max(s, axis=1, keepdims=True)
                    p = jnp.exp(s - m)
                    den = jnp.sum(p, axis=1, keepdims=True)
                    probs.append(p.astype(BF16))
                    dens.append(den)
                    lses.append(m + jnp.log(den))
                for (c, n0), p, den, lse in zip(units, probs, dens, lses):
                    o = jnp.dot(p, vc[pl.ds(c * pad + n0, 2 * blk), :], preferred_element_type=F32) / den
                    out_rows = pl.ds(rr * classes + c + n0 * d, blk, stride=d) if d > 1 else pl.ds(n0, blk)
                    ob[g, out_rows, :] = o
                    lb[g, out_rows, :] = jnp.broadcast_to(lse, (blk, LANES))
                return carry2

            lax.fori_loop(0, n_blocks // per_iter, blocks, 0)
            return carry

        lax.fori_loop(0, d // classes, class_body, 0)

    merge_rows = min(S, 512)

    def merge(t, carry):
        rows = pl.ds(pl.multiple_of(t * merge_rows, merge_rows), merge_rows)
        lses = [lb[g, rows, :] for g in range(len(patterns))]
        m = functools.reduce(jnp.maximum, lses)
        es = [jnp.exp(l - m) for l in lses]
        tot = functools.reduce(lambda a, b: a + b, es)
        acc = functools.reduce(lambda a, b: a + b, [e * ob[g, rows, :] for g, e in enumerate(es)])
        o_ref[0, rows, cols] = (acc / tot).astype(o_ref.dtype)
        return carry

    lax.fori_loop(0, S // merge_rows, merge, 0)


def _dilated_attention(qk, v, B, S, D, patterns, *, unroll=4):
    H = D // LANES
    for window, d in patterns:
        assert S % window == 0
    max_blk = max(w // d for w, d in patterns)
    head = lambda off: pl.BlockSpec((1, S, LANES), lambda b, h: (off + h, b, 0))
    return pl.pallas_call(
        functools.partial(_dilated_fused_body, patterns=patterns, unroll=unroll, heads=1),
        grid=(B, H),
        in_specs=[head(0), head(H), head(0)],
        out_specs=pl.BlockSpec((1, S, LANES), lambda b, h: (b, 0, h)),
        out_shape=jax.ShapeDtypeStruct((B, S, D), BF16),
        scratch_shapes=[
            pltpu.VMEM((S, LANES), F32),
            pltpu.VMEM((S, LANES), F32),
            pltpu.VMEM((S, LANES), F32),
            pltpu.VMEM((S, LANES), BF16),
            pltpu.VMEM((S + unroll * max_blk, LANES), BF16),
            pltpu.VMEM((S + unroll * max_blk, LANES), BF16),
            pltpu.VMEM((len(patterns), S, LANES), F32),
            pltpu.VMEM((len(patterns), S, LANES), F32),
        ],
        compiler_params=_params("parallel", "parallel"),
        name="dilated_attention",
    )(qk, qk, v)


def _rope_tables(S, hd, q_scale):
    pos = jnp.arange(S, dtype=F32)
    inv = ROPE_THETA ** (-jnp.arange(0, hd, 2, dtype=F32) / hd)
    ang = pos[:, None] * inv[None, :]
    ang = jnp.concatenate([ang, ang], axis=-1)
    sign = jnp.concatenate([-jnp.ones((hd // 2,), F32), jnp.ones((hd // 2,), F32)])
    cos, sin = jnp.cos(ang), jnp.sin(ang) * sign
    return jnp.stack([cos * q_scale, cos]), jnp.stack([sin * q_scale, sin])


def _mixer_dilated(xf, xb, wqkv, wo, alpha, B, S, D, narrow=()):
    hd = D // N_HEADS
    assert hd == LANES
    cos_tab, sin_tab = _rope_tables(S, hd, hd**-0.5)
    qk = _matmul(xb, *wqkv, n_cols=2 * D, out_dtype=BF16, rope=(cos_tab, sin_tab, S), narrow=narrow, head_major=True)
    narrowed = ()
    if narrow:
        qk, narrowed = qk
    if wo is None:
        wo = (narrowed[-1][None], 0)
    v = _matmul(xb, *wqkv, n_cols=D, col_offset=2 * D, out_dtype=BF16, head_major=True)
    o = _dilated_attention(qk, v, B, S, D, DILATED_PATTERNS)
    out = _matmul(o.reshape(B * S, D), *wo, n_cols=D, out_dtype=F32, residual=xf, alpha=alpha)
    return out, narrowed


def _pool_ln_body(x_ref, xh_ref, w_ref, sc_ref, g_ref, b_ref, of_ref, ob_ref, xs_ref, y_ref, *, alpha, ts):
    i = pl.program_id(1)
    x = x_ref[0]
    xs_ref[0:HALO_ROWS, :] = jnp.where(i > 0, xh_ref[0], 0.0)
    xs_ref[HALO_ROWS:, :] = x
    cg = x.shape[1] // len(POOL_WINDOWS)
    t1 = (i * ts + 1 + lax.broadcasted_iota(jnp.int32, (ts, 1), 0)).astype(F32)
    for g, win in enumerate(POOL_WINDOWS):
        cs = slice(g * cg, (g + 1) * cg)
        tot = x[:, cs]
        for j in range(1, win):
            tot = tot + xs_ref[pl.ds(HALO_ROWS - j, ts), cs]
        pooled = tot / jnp.minimum(t1, float(win)) - x[:, cs]
        y = jnp.dot(pooled.astype(BF16), w_ref[g], preferred_element_type=F32)
        y_ref[:, cs] = y * sc_ref[:, cs]
    out = _layer_norm_rows(alpha * x + y_ref[...], g_ref[...], b_ref[...])
    of_ref[0] = out
    ob_ref[0] = out.astype(BF16)


def _mixer_pool_ln(xf, wgrp, scale, g, b, alpha, B, S, D, *, ts=256):
    ts = _tile(S, ts)
    assert ts % HALO_ROWS == 0 and max(POOL_WINDOWS) <= HALO_ROWS
    G, cg, _ = wgrp.shape
    x3 = xf.reshape(B, S, D)
    row = pl.BlockSpec((1, ts, D), lambda bb, i: (bb, i, 0))
    halo = pl.BlockSpec((1, HALO_ROWS, D), lambda bb, i: (bb, jnp.maximum(i * (ts // HALO_ROWS) - 1, 0), 0))
    vec = pl.BlockSpec((1, D), lambda bb, i: (0, 0))
    of, ob = pl.pallas_call(
        functools.partial(_pool_ln_body, alpha=alpha, ts=ts),
        grid=(B, S // ts),
        in_specs=[row, halo, pl.BlockSpec((G, cg, cg), lambda bb, i: (0, 0, 0)), vec, vec, vec],
        out_specs=[row, row],
        out_shape=[jax.ShapeDtypeStruct((B, S, D), F32), jax.ShapeDtypeStruct((B, S, D), BF16)],
        scratch_shapes=[pltpu.VMEM((ts + HALO_ROWS, D), F32), pltpu.VMEM((ts, D), F32)],
        compiler_params=_params("parallel", "parallel"),
        name="pool_ln",
    )(x3, x3, wgrp.astype(BF16), scale.reshape(1, D), g.reshape(1, D), b.reshape(1, D))
    return of.reshape(B * S, D), ob.reshape(B * S, D)


def _fgate_body(z_ref, bf_ref, c_ref):
    z = z_ref[0] + bf_ref[...]
    c = jnp.minimum(z, 0.0) - jnp.log1p(jnp.exp(-jnp.abs(z)))
    S = c.shape[0]
    row = lax.broadcasted_iota(jnp.int32, c.shape, 0)
    shift = 1
    while shift < S:
        c = c + jnp.where(row >= shift, pltpu.roll(c, shift, 0), 0.0)
        shift *= 2
    c_ref[0] = c


def _forget_cumsum(z, bias_row):
    B, S, W = z.shape
    blk = pl.BlockSpec((1, S, W), lambda b: (b, 0, 0))
    return pl.pallas_call(
        _fgate_body,
        grid=(B,),
        in_specs=[blk, pl.BlockSpec((1, W), lambda b: (0, 0))],
        out_specs=blk,
        out_shape=jax.ShapeDtypeStruct((B, S, W), F32),
        compiler_params=_params("parallel"),
        name="forget_cumsum",
    )(z, bias_row)


def _fox_body(q_ref, k_ref, v_ref, cc_ref, o_ref, qa_ref, ka_ref, *, bq, bk, heads):
    S = q_ref.shape[1]
    diag_steps = bq // bk
    lane = lax.broadcasted_iota(jnp.int32, (S, LANES), 1)
    for t in range(heads):
        head_lane = pl.program_id(1) * heads + t
        c = jnp.sum(jnp.where(lane == head_lane, cc_ref[0], 0.0), axis=1, keepdims=True) * LOG2_E
        hi = c.astype(BF16).astype(F32)
        mid = (c - hi).astype(BF16).astype(F32)
        lo = c - hi - mid

        def pieces(sign, first, hi=hi, mid=mid, lo=lo):
            return (
                jnp.where(lane == first, sign * hi, 0.0)
                + jnp.where(lane == first + 1, sign * mid, 0.0)
                + jnp.where(lane == first + 2, sign * lo, 0.0)
            )

        qa_ref[t] = (pieces(1.0, 0) + jnp.where((lane >= 3) & (lane < 6), 1.0, 0.0)).astype(BF16)
        ka_ref[t] = (jnp.where(lane < 3, 1.0, 0.0) + pieces(-1.0, 3)).astype(BF16)

    key_minus_query = lax.broadcasted_iota(jnp.int32, (bq, bk), 1) - lax.broadcasted_iota(jnp.int32, (bq, bk), 0)
    nt = (((1,), (1,)), ((), ()))
    across = lambda t: jnp.concatenate([t] * (bk // LANES), axis=1)
    cols = [slice(t * LANES, (t + 1) * LANES) for t in range(heads)]

    def q_block(i, carry):
        r0 = pl.multiple_of(i * bq, bq)
        rows = pl.ds(r0, bq)
        qs = [jnp.concatenate([q_ref[t, rows, :], qa_ref[t, rows, :]], axis=1) for t in range(heads)]

        def k_step(c0, states, diag_offset):
            keys = pl.ds(c0, bk)
            scores = [
                lax.dot_general(
                    q, jnp.concatenate([k_ref[t, keys, :], ka_ref[t, keys, :]], axis=1), nt, preferred_element_type=F32
                )
                for t, q in enumerate(qs)
            ]
            new_states = []
            for t, (s, (m, l, acc)) in enumerate(zip(scores, states)):
                if diag_offset is not None:
                    s = jnp.where(key_minus_query <= -diag_offset, s, MASK_VALUE)
                m_new = jnp.maximum(m, jnp.max(s, axis=1, keepdims=True))
                a = jnp.exp2(m - m_new)
                p = jnp.exp2(s - across(m_new))
                l = a * l + jnp.sum(p, axis=1, keepdims=True)
                acc = a * acc + jnp.dot(p.astype(BF16), v_ref[t, keys, :], preferred_element_type=F32)
                new_states.append((m_new, l, acc))
            return tuple(new_states)

        init = (jnp.full((bq, LANES), MASK_VALUE, F32), jnp.zeros((bq, LANES), F32), jnp.zeros((bq, LANES), F32))
        states = lax.fori_loop(
            0, i * diag_steps, lambda j, st: k_step(pl.multiple_of(j * bk, bk), st, None), (init,) * heads
        )
        for jj in range(diag_steps):
            states = k_step(pl.multiple_of(r0 + jj * bk, bk), states, jj * bk)
        for (_, l, acc), cs in zip(states, cols):
            o_ref[0, rows, cs] = (acc / l).astype(o_ref.dtype)
        return carry

    lax.fori_loop(0, S // bq, q_block, 0)


def _fox_attention(proj, c_col, B, S, D, *, bq=1024, bk=1024, heads=2):
    bq = _tile(S, bq)
    bk = _tile(bq, bk)
    W = heads * LANES
    groups = D // W
    part = lambda p: pl.BlockSpec((heads, S, LANES), lambda b, g: (p * groups + g, b, 0))
    return pl.pallas_call(
        functools.partial(_fox_body, bq=bq, bk=bk, heads=heads),
        grid=(B, groups),
        in_specs=[part(0), part(1), part(2), pl.BlockSpec((1, S, LANES), lambda b, g: (b, 0, 0))],
        out_specs=pl.BlockSpec((1, S, W), lambda b, g: (b, 0, g)),
        out_shape=jax.ShapeDtypeStruct((B, S, D), BF16),
        scratch_shapes=[pltpu.VMEM((heads, S, LANES), BF16), pltpu.VMEM((heads, S, LANES), BF16)],
        compiler_params=_params("parallel", "arbitrary"),
        name="fox_attention",
    )(proj, proj, proj, c_col)


def _mixer_fox(xf, xb, win, bf, wo, alpha, B, S, D):
    H = N_HEADS
    assert D // H == LANES and H <= LANES
    col_scale = jnp.concatenate([jnp.full((D,), LANES**-0.5 * LOG2_E, F32), jnp.ones((2 * D,), F32)])
    proj = _matmul(xb, *win, n_cols=3 * D, out_dtype=BF16, col_scale=col_scale, head_major=True)
    w_gate = jnp.pad(win[0][win[1], :, 3 * D :], ((0, 0), (0, LANES - H)))[None]
    z = _matmul(xb, w_gate, 0, n_cols=LANES, out_dtype=F32)
    bias_row = jnp.pad(bf.astype(F32), (0, LANES - H)).reshape(1, LANES)
    c_col = _forget_cumsum(z.reshape(B, S, LANES), bias_row)
    o = _fox_attention(proj, c_col, B, S, D)
    return _matmul(o.reshape(B * S, D), *wo, n_cols=D, out_dtype=F32, residual=xf, alpha=alpha)


def _convgate_body(b_ref, c_ref, h_ref, ch_ref, hh_ref, w_ref, o_ref, zs_ref, *, ts):
    i = pl.program_id(1)
    z = c_ref[0].astype(F32) * h_ref[0].astype(F32)
    zh = ch_ref[0].astype(F32) * hh_ref[0].astype(F32)
    zs_ref[0:HALO_ROWS, :] = jnp.where(i > 0, zh, 0.0)
    zs_ref[HALO_ROWS:, :] = z
    w = w_ref[...]
    conv = w[CONV_WIDTH - 1 : CONV_WIDTH] * z
    for j in range(1, CONV_WIDTH):
        tap = CONV_WIDTH - 1 - j
        conv = conv + w[tap : tap + 1] * zs_ref[pl.ds(HALO_ROWS - j, ts), :]
    o_ref[0] = (b_ref[0].astype(F32) * conv).astype(o_ref.dtype)


def _conv_gate(proj, wconv, B, S, D, *, ts=512, tc=1024):
    ts, tc = _tile(S, ts), _tile(D, tc)
    assert ts % HALO_ROWS == 0 and CONV_WIDTH - 1 <= HALO_ROWS
    nct = D // tc
    cur = lambda part: pl.BlockSpec((1, ts, tc), lambda b, i, j: (b, i, part * nct + j))
    halo = lambda part: pl.BlockSpec(
        (1, HALO_ROWS, tc), lambda b, i, j: (b, jnp.maximum(i * (ts // HALO_ROWS) - 1, 0), part * nct + j)
    )
    return pl.pallas_call(
        functools.partial(_convgate_body, ts=ts),
        grid=(B, S // ts, nct),
        in_specs=[cur(0), cur(1), cur(2), halo(1), halo(2), pl.BlockSpec((CONV_WIDTH, tc), lambda b, i, j: (0, j))],
        out_specs=pl.BlockSpec((1, ts, tc), lambda b, i, j: (b, i, j)),
        out_shape=jax.ShapeDtypeStruct((B, S, D), BF16),
        scratch_shapes=[pltpu.VMEM((ts + HALO_ROWS, tc), F32)],
        compiler_params=_params("parallel", "parallel", "parallel"),
        name="conv_gate",
    )(proj, proj, proj, proj, proj, wconv.astype(F32))


def _mixer_conv(xf, xb, win, wconv, wout, alpha, B, S, D):
    proj = _matmul(xb, *win, n_cols=3 * D, out_dtype=BF16)
    u = _conv_gate(proj.reshape(B, S, 3 * D), wconv, B, S, D)
    return _matmul(u.reshape(B * S, D), *wout, n_cols=D, out_dtype=F32, residual=xf, alpha=alpha)


def kernel(x, ln_g, ln_b, mlp_w1, mlp_w2, a_wqkv, a_wo, b_wgrp, b_scale, c_win, c_bf, c_wo, d_win, d_conv, d_wout):
    B, S, D = x.shape
    depth = ln_g.shape[0]
    n_mixers = 4
    alpha = (2.0 * depth) ** 0.25
    xf = x.reshape(B * S, D).astype(F32)
    xb = xf.astype(BF16)

    mixer_weights = {0: (a_wqkv, a_wo), 2: (c_win, c_wo), 3: (d_win, d_wout)}
    narrowed = {}

    def upcoming(i):
        if i >= depth:
            return []
        mats = [(w, i // n_mixers) for w in mixer_weights.get(i % n_mixers, ())]
        return mats + [(mlp_w1, i)]

    def weight(w, layer):
        b = narrowed.get((id(w), layer))
        return (w, layer) if b is None else (b[None], 0)

    def record(sides, results):
        for (w, layer), b in zip(sides, results):
            narrowed[(id(w), layer)] = b

    for i in range(depth):
        kind, j = i % n_mixers, i // n_mixers
        if kind == 1:
            xf, xb = _mixer_pool_ln(xf, b_wgrp[j], b_scale[j], ln_g[i, 0], ln_b[i, 0], alpha, B, S, D)
        else:
            if kind == 0:
                sides = [s for s in [(mlp_w1, i), (a_wo, j)] if (id(s[0]), s[1]) not in narrowed]
                wo = None if sides and sides[-1][0] is a_wo else weight(a_wo, j)
                v, results = _mixer_dilated(xf, xb, weight(a_wqkv, j), wo, alpha, B, S, D, narrow=sides)
                record(sides, results)
            elif kind == 2:
                v = _mixer_fox(xf, xb, weight(c_win, j), c_bf[j], weight(c_wo, j), alpha, B, S, D)
            else:
                v = _mixer_conv(xf, xb, weight(d_win, j), d_conv[j], weight(d_wout, j), alpha, B, S, D)
            xf, xb = _layer_norm(v, ln_g[i, 0], ln_b[i, 0])
        sides = [(mlp_w2, i)] + upcoming(i + 1)
        hid, results = _matmul(xb, *weight(mlp_w1, i), n_cols=mlp_w1.shape[2], out_dtype=BF16, act="relu2", narrow=sides)
        record(sides, results)
        v = _matmul_bf16w(hid, *weight(mlp_w2, i), xf, alpha)
        xf, xb = _layer_norm(v, ln_g[i, 1], ln_b[i, 1])
    return xf.reshape(B, S, D).astype(x.dtype)
```

```python
import functools

import jax
import jax.numpy as jnp
from jax import lax
from jax.experimental import pallas as pl
from jax.experimental.pallas import tpu as pltpu

N_HEADS = 32
ROPE_THETA = 10000.0
DILATED_PATTERNS = ((128, 1), (512, 4), (2048, 16))
POOL_WINDOWS = (2, 4, 8, 16)
CONV_WIDTH = 3
LN_EPS = 1e-5
LANES = 128
MXU_WIDTH = 256
HALO_ROWS = 16
VMEM_LIMIT_BYTES = 56 * 1024 * 1024
VMEM_LIMIT_LARGE_BYTES = 60000 * 1024
MASK_VALUE = -1e30
LOG2_E = 1.4426950408889634

BF16 = jnp.bfloat16
F32 = jnp.float32


def _tile(n, pref):
    t = min(n, pref)
    while n % t:
        assert t % 2 == 0, (n, pref)
        t //= 2
    return t


def _params(*sem, vmem_limit_bytes=VMEM_LIMIT_BYTES):
    return pltpu.CompilerParams(dimension_semantics=sem, vmem_limit_bytes=vmem_limit_bytes)


def _mm_acc_body(x_ref, w_ref, r_ref, o_ref, *, alpha):
    k = pl.program_id(2)
    x = x_ref[...]
    width = min(o_ref.shape[1], MXU_WIDTH)
    for c in range(o_ref.shape[1] // width):
        cs = slice(c * width, (c + 1) * width)
        acc = jnp.dot(x, w_ref[:, cs], preferred_element_type=F32)
        o_ref[:, cs] = jnp.where(k == 0, alpha * r_ref[:, cs], o_ref[:, cs]) + acc


def _matmul_bf16w(x, w, layer, residual, alpha, *, bm=1024, bn=1024, bk=4096):
    M, K = x.shape
    N = w.shape[2]
    bm, bn, bk = _tile(M, bm), _tile(N, bn), _tile(K, bk)
    out_spec = pl.BlockSpec((bm, bn), lambda i, j, k: (i, j))
    return pl.pallas_call(
        functools.partial(_mm_acc_body, alpha=alpha),
        grid=(M // bm, N // bn, K // bk),
        in_specs=[
            pl.BlockSpec((bm, bk), lambda i, j, k: (i, k)),
            pl.BlockSpec((None, bk, bn), lambda i, j, k: (layer, k, j)),
            out_spec,
        ],
        out_specs=out_spec,
        out_shape=jax.ShapeDtypeStruct((M, N), F32),
        compiler_params=_params("parallel", "parallel", "arbitrary", vmem_limit_bytes=VMEM_LIMIT_LARGE_BYTES),
        name="matmul_ktiled",
    )(x, w, residual)


def _mm_body(*refs, act, has_scale, rope, alpha, n_side, narrow_w, head_major):
    x_ref, w_ref = refs[0], refs[1]
    n_extra = 2 if rope else (1 if has_scale else 0)
    n_in = 2 + n_extra + (alpha is not None) + n_side
    o_ref = refs[n_in]
    for t in range(n_side):
        refs[n_in + 1 + t][...] = refs[n_in - n_side + t][...].astype(BF16)

    if narrow_w:
        wb_ref = refs[-1]

        @pl.when(pl.program_id(1) == 0)
        def _():
            wb_ref[...] = w_ref[...].astype(BF16)

        w_ref = wb_ref

    x = x_ref[...]
    n_out = w_ref.shape[1]
    width = min(n_out, MXU_WIDTH)
    for c in range(n_out // width):
        cs = slice(c * width, (c + 1) * width)
        acc = jnp.dot(x, w_ref[:, cs], preferred_element_type=F32)
        if rope:
            cos, sin = refs[2][0], refs[3][0]
            for h in range(width // LANES):
                a = acc[:, h * LANES : (h + 1) * LANES]
                hs = slice(c * width + h * LANES, c * width + (h + 1) * LANES)
                roped = (a * cos + pltpu.roll(a, LANES // 2, 1) * sin).astype(o_ref.dtype)
                if head_major:
                    o_ref[c * (width // LANES) + h] = roped
                else:
                    o_ref[:, hs] = roped
            continue
        if act == "relu2":
            acc = jnp.maximum(acc, 0.0)
            acc = acc * acc
        if has_scale:
            acc = acc * refs[2][:, cs]
        if alpha is not None:
            acc = alpha * refs[2 + n_extra][:, cs] + acc
        if head_major:
            for h in range(width // LANES):
                o_ref[c * (width // LANES) + h] = acc[:, h * LANES : (h + 1) * LANES].astype(o_ref.dtype)
        else:
            o_ref[:, cs] = acc.astype(o_ref.dtype)


def _matmul(x, w, layer, *, n_cols, out_dtype, col_offset=0, act=None, col_scale=None, rope=None, residual=None,
            alpha=None, narrow=(), head_major=False, bm=1024, bn=None):
    M, K = x.shape
    N = n_cols
    narrow_w = w.dtype != BF16
    if bn is None:
        bn = 512 if narrow_w else 1024
    if rope is not None:
        cos_tab, sin_tab, seq = rope
        bm, bn = _tile(seq, bm), _tile(N // 2, bn)
    else:
        bm, bn = _tile(M, bm), _tile(N, bn)
    assert col_offset % bn == 0 and (residual is None) == (alpha is None)
    off = col_offset // bn
    n_i = M // bm
    n_steps = (N // bn) * n_i
    out_spec = pl.BlockSpec((bm, bn), lambda j, i: (i, j))
    in_specs = [
        pl.BlockSpec((bm, K), lambda j, i: (i, 0)),
        pl.BlockSpec((None, K, bn), lambda j, i: (layer, 0, j + off)),
    ]
    args = [x, w]
    if rope is not None:
        tiles_per_part, seq_tiles = (N // 2) // bn, seq // bm
        tab_spec = pl.BlockSpec((1, bm, LANES), lambda j, i: (j // tiles_per_part, i % seq_tiles, 0))
        in_specs += [tab_spec, tab_spec]
        args += [cos_tab, sin_tab]
    elif col_scale is not None:
        in_specs.append(pl.BlockSpec((1, bn), lambda j, i: (0, j)))
        args.append(col_scale.reshape(1, N).astype(F32))
    if residual is not None:
        in_specs.append(out_spec)
        args.append(residual)
    out_specs, out_shape = [out_spec], [jax.ShapeDtypeStruct((M, N), out_dtype)]
    if head_major:
        assert residual is None and bn % LANES == 0
        out_specs = [pl.BlockSpec((bn // LANES, bm, LANES), lambda j, i: (j, i, 0))]
        out_shape = [jax.ShapeDtypeStruct((N // LANES, M, LANES), out_dtype)]
    for src, src_layer in narrow:
        rows, width = src.shape[1], src.shape[2]
        slab = rows // n_steps
        assert slab * n_steps == rows and slab % HALO_ROWS == 0
        in_specs.append(pl.BlockSpec((None, slab, width), lambda j, i, src_layer=src_layer: (src_layer, j * n_i + i, 0)))
        args.append(src)
        out_specs.append(pl.BlockSpec((slab, width), lambda j, i: (j * n_i + i, 0)))
        out_shape.append(jax.ShapeDtypeStruct((rows, width), BF16))
    outs = pl.pallas_call(
        functools.partial(
            _mm_body, act=act, has_scale=col_scale is not None, rope=rope is not None, alpha=alpha,
            n_side=len(narrow), narrow_w=narrow_w, head_major=head_major,
        ),
        grid=(N // bn, n_i),
        in_specs=in_specs,
        out_specs=out_specs,
        out_shape=out_shape,
        scratch_shapes=[pltpu.VMEM((K, bn), BF16)] if narrow_w else [],
        compiler_params=_params("arbitrary", "arbitrary"),
        name="matmul",
    )(*args)
    return (outs[0], outs[1:]) if narrow else outs[0]


def _layer_norm_rows(v, g, b):
    mu = jnp.mean(v, axis=-1, keepdims=True)
    xc = v - mu
    var = jnp.mean(xc * xc, axis=-1, keepdims=True)
    return xc * lax.rsqrt(var + LN_EPS) * g + b


def _ln_body(v_ref, g_ref, b_ref, of_ref, ob_ref):
    out = _layer_norm_rows(v_ref[...], g_ref[...], b_ref[...])
    of_ref[...] = out
    ob_ref[...] = out.astype(BF16)


def _layer_norm(v, g, b, *, tm=256):
    M, D = v.shape
    tm = _tile(M, tm)
    row = pl.BlockSpec((tm, D), lambda i: (i, 0))
    vec = pl.BlockSpec((1, D), lambda i: (0, 0))
    return pl.pallas_call(
        _ln_body,
        grid=(M // tm,),
        in_specs=[row, vec, vec],
        out_specs=[row, row],
        out_shape=[jax.ShapeDtypeStruct((M, D), F32), jax.ShapeDtypeStruct((M, D), BF16)],
        compiler_params=_params("parallel"),
        name="layer_norm",
    )(v, g.reshape(1, D), b.reshape(1, D))


def _dilated_fused_body(q_ref, k_ref, v_ref, o_ref, *scratch, patterns, unroll, heads):
    for t in range(heads):
        _dilated_head(q_ref, k_ref, v_ref, o_ref, *scratch, cols=slice(t * LANES, (t + 1) * LANES), patterns=patterns,
                      unroll=unroll)


def _dilated_head(q_ref, k_ref, v_ref, o_ref, q32, k32, v32, qc, kc, vc, ob, lb, *, cols, patterns, unroll):
    S = q_ref.shape[1]
    q32[...] = q_ref[0, :, cols].astype(F32)
    k32[...] = k_ref[0, :, cols].astype(F32)
    v32[...] = v_ref[0, :, cols].astype(F32)
    nt = (((1,), (1,)), ((), ()))
    for g, (window, d) in enumerate(patterns):
        blk = window // d
        sc = S // d
        n_blocks = sc // blk
        per_iter = min(n_blocks, unroll)
        classes = min(d, max(1, unroll // n_blocks))
        pad = sc + blk
        for c in range(classes):
            kc[c * pad : c * pad + blk, :] = jnp.zeros((blk, LANES), BF16)
            vc[c * pad : c * pad + blk, :] = jnp.zeros((blk, LANES), BF16)
        qi = lax.broadcasted_iota(jnp.int32, (blk, 2 * blk), 0)
        kj = lax.broadcasted_iota(jnp.int32, (blk, 2 * blk), 1)

        def class_body(rr, carry, g=g, d=d, blk=blk, sc=sc, n_blocks=n_blocks, per_iter=per_iter, classes=classes,
                       pad=pad, qi=qi, kj=kj):
            for c in range(classes):
                if d == 1:
                    qc[0:sc, :] = q_ref[0, :, cols]
                    kc[blk:pad, :] = k_ref[0, :, cols]
                    vc[blk:pad, :] = v_ref[0, :, cols]
                else:
                    rows = pl.ds(rr * classes + c, sc, stride=d)
                    qc[c * sc : (c + 1) * sc, :] = q32[rows, :].astype(BF16)
                    kc[c * pad + blk : (c + 1) * pad, :] = k32[rows, :].astype(BF16)
                    vc[c * pad + blk : (c + 1) * pad, :] = v32[rows, :].astype(BF16)

            def blocks(it, carry2):
                units = [
                    (c, pl.multiple_of((it * per_iter + u) * blk, blk)) for c in range(classes) for u in range(per_iter)
                ]
                scores = [
                    lax.dot_general(
                        qc[pl.ds(c * sc + n0, blk), :],
                        kc[pl.ds(c * pad + n0, 2 * blk), :],
                        nt,
                        preferred_element_type=F32,
                    )
                    for c, n0 in units
                ]
                probs, dens, lses = [], [], []
                for (c, n0), s in zip(units, scores):
                    visible = (kj >= jnp.maximum(qi, blk - n0)) & (kj <= qi + blk)
                    s = jnp.where(visible, s, MASK_VALUE)
                    m = jnp.max(s, axis=1, keepdims=True)
                    p = jnp.exp(s - m)
                    den = jnp.sum(p, axis=1, keepdims=True)
                    probs.append(p.astype(BF16))
                    dens.append(den)
                    lses.append(m + jnp.log(den))
                for (c, n0), p, den, lse in zip(units, probs, dens, lses):
                    o = jnp.dot(p, vc[pl.ds(c * pad + n0, 2 * blk), :], preferred_element_type=F32) / den
                    out_rows = pl.ds(rr * classes + c + n0 * d, blk, stride=d) if d > 1 else pl.ds(n0, blk)
                    ob[g, out_rows, :] = o
                    lb[g, out_rows, :] = jnp.broadcast_to(lse, (blk, LANES))
                return carry2

            lax.fori_loop(0, n_blocks // per_iter, blocks, 0)
            return carry

        lax.fori_loop(0, d // classes, class_body, 0)

    merge_rows = min(S, 512)

    def merge(t, carry):
        rows = pl.ds(pl.multiple_of(t * merge_rows, merge_rows), merge_rows)
        lses = [lb[g, rows, :] for g in range(len(patterns))]
        m = functools.reduce(jnp.maximum, lses)
        es = [jnp.exp(l - m) for l in lses]
        tot = functools.reduce(lambda a, b: a + b, es)
        acc = functools.reduce(lambda a, b: a + b, [e * ob[g, rows, :] for g, e in enumerate(es)])
        o_ref[0, rows, cols] = (acc / tot).astype(o_ref.dtype)
        return carry

    lax.fori_loop(0, S // merge_rows, merge, 0)


def _dilated_attention(qk, v, B, S, D, patterns, *, unroll=8):
    H = D // LANES
    for window, d in patterns:
        assert S % window == 0
    max_blk = max(w // d for w, d in patterns)
    head = lambda off: pl.BlockSpec((1, S, LANES), lambda b, h: (off + h, b, 0))
    return pl.pallas_call(
        functools.partial(_dilated_fused_body, patterns=patterns, unroll=unroll, heads=1),
        grid=(B, H),
        in_specs=[head(0), head(H), head(0)],
        out_specs=pl.BlockSpec((1, S, LANES), lambda b, h: (b, 0, h)),
        out_shape=jax.ShapeDtypeStruct((B, S, D), BF16),
        scratch_shapes=[
            pltpu.VMEM((S, LANES), F32),
            pltpu.VMEM((S, LANES), F32),
            pltpu.VMEM((S, LANES), F32),
            pltpu.VMEM((S, LANES), BF16),
            pltpu.VMEM((S + unroll * max_blk, LANES), BF16),
            pltpu.VMEM((S + unroll * max_blk, LANES), BF16),
            pltpu.VMEM((len(patterns), S, LANES), F32),
            pltpu.VMEM((len(patterns), S, LANES), F32),
        ],
        compiler_params=_params("parallel", "parallel"),
        name="dilated_attention",
    )(qk, qk, v)


def _rope_tables(S, hd, q_scale):
    pos = jnp.arange(S, dtype=F32)
    inv = ROPE_THETA ** (-jnp.arange(0, hd, 2, dtype=F32) / hd)
    ang = pos[:, None] * inv[None, :]
    ang = jnp.concatenate([ang, ang], axis=-1)
    sign = jnp.concatenate([-jnp.ones((hd // 2,), F32), jnp.ones((hd // 2,), F32)])
    cos, sin = jnp.cos(ang), jnp.sin(ang) * sign
    return jnp.stack([cos * q_scale, cos]), jnp.stack([sin * q_scale, sin])


def _mixer_dilated(xf, xb, wqkv, wo, alpha, B, S, D, narrow=()):
    hd = D // N_HEADS
    assert hd == LANES
    cos_tab, sin_tab = _rope_tables(S, hd, hd**-0.5)
    qk = _matmul(xb, *wqkv, n_cols=2 * D, out_dtype=BF16, rope=(cos_tab, sin_tab, S), narrow=narrow, head_major=True)
    narrowed = ()
    if narrow:
        qk, narrowed = qk
    if wo is None:
        wo = (narrowed[-1][None], 0)
    v = _matmul(xb, *wqkv, n_cols=D, col_offset=2 * D, out_dtype=BF16, head_major=True)
    o = _dilated_attention(qk, v, B, S, D, DILATED_PATTERNS)
    out = _matmul(o.reshape(B * S, D), *wo, n_cols=D, out_dtype=F32, residual=xf, alpha=alpha)
    return out, narrowed


def _pool_ln_body(x_ref, xh_ref, w_ref, sc_ref, g_ref, b_ref, of_ref, ob_ref, xs_ref, y_ref, *, alpha, ts):
    i = pl.program_id(1)
    x = x_ref[0]
    xs_ref[0:HALO_ROWS, :] = jnp.where(i > 0, xh_ref[0], 0.0)
    xs_ref[HALO_ROWS:, :] = x
    cg = x.shape[1] // len(POOL_WINDOWS)
    t1 = (i * ts + 1 + lax.broadcasted_iota(jnp.int32, (ts, 1), 0)).astype(F32)
    for g, win in enumerate(POOL_WINDOWS):
        cs = slice(g * cg, (g + 1) * cg)
        tot = x[:, cs]
        for j in range(1, win):
            tot = tot + xs_ref[pl.ds(HALO_ROWS - j, ts), cs]
        pooled = tot / jnp.minimum(t1, float(win)) - x[:, cs]
        y = jnp.dot(pooled.astype(BF16), w_ref[g], preferred_element_type=F32)
        y_ref[:, cs] = y * sc_ref[:, cs]
    out = _layer_norm_rows(alpha * x + y_ref[...], g_ref[...], b_ref[...])
    of_ref[0] = out
    ob_ref[0] = out.astype(BF16)


def _mixer_pool_ln(xf, wgrp, scale, g, b, alpha, B, S, D, *, ts=256):
    ts = _tile(S, ts)
    assert ts % HALO_ROWS == 0 and max(POOL_WINDOWS) <= HALO_ROWS
    G, cg, _ = wgrp.shape
    x3 = xf.reshape(B, S, D)
    row = pl.BlockSpec((1, ts, D), lambda bb, i: (bb, i, 0))
    halo = pl.BlockSpec((1, HALO_ROWS, D), lambda bb, i: (bb, jnp.maximum(i * (ts // HALO_ROWS) - 1, 0), 0))
    vec = pl.BlockSpec((1, D), lambda bb, i: (0, 0))
    of, ob = pl.pallas_call(
        functools.partial(_pool_ln_body, alpha=alpha, ts=ts),
        grid=(B, S // ts),
        in_specs=[row, halo, pl.BlockSpec((G, cg, cg), lambda bb, i: (0, 0, 0)), vec, vec, vec],
        out_specs=[row, row],
        out_shape=[jax.ShapeDtypeStruct((B, S, D), F32), jax.ShapeDtypeStruct((B, S, D), BF16)],
        scratch_shapes=[pltpu.VMEM((ts + HALO_ROWS, D), F32), pltpu.VMEM((ts, D), F32)],
        compiler_params=_params("parallel", "parallel"),
        name="pool_ln",
    )(x3, x3, wgrp.astype(BF16), scale.reshape(1, D), g.reshape(1, D), b.reshape(1, D))
    return of.reshape(B * S, D), ob.reshape(B * S, D)


def _fgate_body(z_ref, bf_ref, c_ref):
    z = z_ref[0] + bf_ref[...]
    c = jnp.minimum(z, 0.0) - jnp.log1p(jnp.exp(-jnp.abs(z)))
    S = c.shape[0]
    row = lax.broadcasted_iota(jnp.int32, c.shape, 0)
    shift = 1
    while shift < S:
        c = c + jnp.where(row >= shift, pltpu.roll(c, shift, 0), 0.0)
        shift *= 2
    c_ref[0] = c


def _forget_cumsum(z, bias_row):
    B, S, W = z.shape
    blk = pl.BlockSpec((1, S, W), lambda b: (b, 0, 0))
    return pl.pallas_call(
        _fgate_body,
        grid=(B,),
        in_specs=[blk, pl.BlockSpec((1, W), lambda b: (0, 0))],
        out_specs=blk,
        out_shape=jax.ShapeDtypeStruct((B, S, W), F32),
        compiler_params=_params("parallel"),
        name="forget_cumsum",
    )(z, bias_row)


def _fox_body(q_ref, k_ref, v_ref, cc_ref, o_ref, qa_ref, ka_ref, *, bq, bk, heads):
    S = q_ref.shape[1]
    diag_steps = bq // bk
    lane = lax.broadcasted_iota(jnp.int32, (S, LANES), 1)
    for t in range(heads):
        head_lane = pl.program_id(1) * heads + t
        c = jnp.sum(jnp.where(lane == head_lane, cc_ref[0], 0.0), axis=1, keepdims=True) * LOG2_E
        hi = c.astype(BF16).astype(F32)
        mid = (c - hi).astype(BF16).astype(F32)
        lo = c - hi - mid

        def pieces(sign, first, hi=hi, mid=mid, lo=lo):
            return (
                jnp.where(lane == first, sign * hi, 0.0)
                + jnp.where(lane == first + 1, sign * mid, 0.0)
                + jnp.where(lane == first + 2, sign * lo, 0.0)
            )

        qa_ref[t] = (pieces(1.0, 0) + jnp.where((lane >= 3) & (lane < 6), 1.0, 0.0)).astype(BF16)
        ka_ref[t] = (jnp.where(lane < 3, 1.0, 0.0) + pieces(-1.0, 3)).astype(BF16)

    key_minus_query = lax.broadcasted_iota(jnp.int32, (bq, bk), 1) - lax.broadcasted_iota(jnp.int32, (bq, bk), 0)
    nt = (((1,), (1,)), ((), ()))
    across = lambda t: jnp.concatenate([t] * (bk // LANES), axis=1)
    cols = [slice(t * LANES, (t + 1) * LANES) for t in range(heads)]

    def q_block(i, carry):
        r0 = pl.multiple_of(i * bq, bq)
        rows = pl.ds(r0, bq)
        qs = [jnp.concatenate([q_ref[t, rows, :], qa_ref[t, rows, :]], axis=1) for t in range(heads)]

        def k_step(c0, states, diag_offset):
            keys = pl.ds(c0, bk)
            scores = [
                lax.dot_general(
                    q, jnp.concatenate([k_ref[t, keys, :], ka_ref[t, keys, :]], axis=1), nt, preferred_element_type=F32
                )
                for t, q in enumerate(qs)
            ]
            new_states = []
            for t, (s, (m, l, acc)) in enumerate(zip(scores, states)):
                if diag_offset is not None:
                    s = jnp.where(key_minus_query <= -diag_offset, s, MASK_VALUE)
                m_new = jnp.maximum(m, jnp.max(s, axis=1, keepdims=True))
                a = jnp.exp2(m - m_new)
                p = jnp.exp2(s - across(m_new))
                l = a * l + jnp.sum(p, axis=1, keepdims=True)
                acc = a * acc + jnp.dot(p.astype(BF16), v_ref[t, keys, :], preferred_element_type=F32)
                new_states.append((m_new, l, acc))
            return tuple(new_states)

        init = (jnp.full((bq, LANES), MASK_VALUE, F32), jnp.zeros((bq, LANES), F32), jnp.zeros((bq, LANES), F32))
        states = lax.fori_loop(
            0, i * diag_steps, lambda j, st: k_step(pl.multiple_of(j * bk, bk), st, None), (init,) * heads
        )
        for jj in range(diag_steps):
            states = k_step(pl.multiple_of(r0 + jj * bk, bk), states, jj * bk)
        for (_, l, acc), cs in zip(states, cols):
            o_ref[0, rows, cs] = (acc / l).astype(o_ref.dtype)
        return carry

    lax.fori_loop(0, S // bq, q_block, 0)


def _fox_attention(proj, c_col, B, S, D, *, bq=1024, bk=1024, heads=2):
    bq = _tile(S, bq)
    bk = _tile(bq, bk)
    W = heads * LANES
    groups = D // W
    part = lambda p: pl.BlockSpec((heads, S, LANES), lambda b, g: (p * groups + g, b, 0))
    return pl.pallas_call(
        functools.partial(_fox_body, bq=bq, bk=bk, heads=heads),
        grid=(B, groups),
        in_specs=[part(0), part(1), part(2), pl.BlockSpec((1, S, LANES), lambda b, g: (b, 0, 0))],
        out_specs=pl.BlockSpec((1, S, W), lambda b, g: (b, 0, g)),
        out_shape=jax.ShapeDtypeStruct((B, S, D), BF16),
        scratch_shapes=[pltpu.VMEM((heads, S, LANES), BF16), pltpu.VMEM((heads, S, LANES), BF16)],
        compiler_params=_params("parallel", "arbitrary"),
        name="fox_attention",
    )(proj, proj, proj, c_col)


def _mixer_fox(xf, xb, win, bf, wo, alpha, B, S, D):
    H = N_HEADS
    assert D // H == LANES and H <= LANES
    col_scale = jnp.concatenate([jnp.full((D,), LANES**-0.5 * LOG2_E, F32), jnp.ones((2 * D,), F32)])
    proj = _matmul(xb, *win, n_cols=3 * D, out_dtype=BF16, col_scale=col_scale, head_major=True)
    w_gate = jnp.pad(win[0][win[1], :, 3 * D :], ((0, 0), (0, LANES - H)))[None]
    z = _matmul(xb, w_gate, 0, n_cols=LANES, out_dtype=F32)
    bias_row = jnp.pad(bf.astype(F32), (0, LANES - H)).reshape(1, LANES)
    c_col = _forget_cumsum(z.reshape(B, S, LANES), bias_row)
    o = _fox_attention(proj, c_col, B, S, D)
    return _matmul(o.reshape(B * S, D), *wo, n_cols=D, out_dtype=F32, residual=xf, alpha=alpha)


def _convgate_body(b_ref, c_ref, h_ref, ch_ref, hh_ref, w_ref, o_ref, zs_ref, *, ts):
    i = pl.program_id(1)
    z = c_ref[0].astype(F32) * h_ref[0].astype(F32)
    zh = ch_ref[0].astype(F32) * hh_ref[0].astype(F32)
    zs_ref[0:HALO_ROWS, :] = jnp.where(i > 0, zh, 0.0)
    zs_ref[HALO_ROWS:, :] = z
    w = w_ref[...]
    conv = w[CONV_WIDTH - 1 : CONV_WIDTH] * z
    for j in range(1, CONV_WIDTH):
        tap = CONV_WIDTH - 1 - j
        conv = conv + w[tap : tap + 1] * zs_ref[pl.ds(HALO_ROWS - j, ts), :]
    o_ref[0] = (b_ref[0].astype(F32) * conv).astype(o_ref.dtype)


def _conv_gate(proj, wconv, B, S, D, *, ts=512, tc=1024):
    ts, tc = _tile(S, ts), _tile(D, tc)
    assert ts % HALO_ROWS == 0 and CONV_WIDTH - 1 <= HALO_ROWS
    nct = D // tc
    cur = lambda part: pl.BlockSpec((1, ts, tc), lambda b, i, j: (b, i, part * nct + j))
    halo = lambda part: pl.BlockSpec(
        (1, HALO_ROWS, tc), lambda b, i, j: (b, jnp.maximum(i * (ts // HALO_ROWS) - 1, 0), part * nct + j)
    )
    return pl.pallas_call(
        functools.partial(_convgate_body, ts=ts),
        grid=(B, S // ts, nct),
        in_specs=[cur(0), cur(1), cur(2), halo(1), halo(2), pl.BlockSpec((CONV_WIDTH, tc), lambda b, i, j: (0, j))],
        out_specs=pl.BlockSpec((1, ts, tc), lambda b, i, j: (b, i, j)),
        out_shape=jax.ShapeDtypeStruct((B, S, D), BF16),
        scratch_shapes=[pltpu.VMEM((ts + HALO_ROWS, tc), F32)],
        compiler_params=_params("parallel", "parallel", "parallel"),
        name="conv_gate",
    )(proj, proj, proj, proj, proj, wconv.astype(F32))


def _mixer_conv(xf, xb, win, wconv, wout, alpha, B, S, D):
    proj = _matmul(xb, *win, n_cols=3 * D, out_dtype=BF16)
    u = _conv_gate(proj.reshape(B, S, 3 * D), wconv, B, S, D)
    return _matmul(u.reshape(B * S, D), *wout, n_cols=D, out_dtype=F32, residual=xf, alpha=alpha)


def kernel(x, ln_g, ln_b, mlp_w1, mlp_w2, a_wqkv, a_wo, b_wgrp, b_scale, c_win, c_bf, c_wo, d_win, d_conv, d_wout):
    B, S, D = x.shape
    depth = ln_g.shape[0]
    n_mixers = 4
    alpha = (2.0 * depth) ** 0.25
    xf = x.reshape(B * S, D).astype(F32)
    xb = xf.astype(BF16)

    mixer_weights = {0: (a_wqkv, a_wo), 2: (c_win, c_wo), 3: (d_win, d_wout)}
    narrowed = {}

    def upcoming(i):
        if i >= depth:
            return []
        mats = [(w, i // n_mixers) for w in mixer_weights.get(i % n_mixers, ())]
        return mats + [(mlp_w1, i)]

    def weight(w, layer):
        b = narrowed.get((id(w), layer))
        return (w, layer) if b is None else (b[None], 0)

    def record(sides, results):
        for (w, layer), b in zip(sides, results):
            narrowed[(id(w), layer)] = b

    for i in range(depth):
        kind, j = i % n_mixers, i // n_mixers
        if kind == 1:
            xf, xb = _mixer_pool_ln(xf, b_wgrp[j], b_scale[j], ln_g[i, 0], ln_b[i, 0], alpha, B, S, D)
        else:
            if kind == 0:
                sides = [s for s in [(mlp_w1, i), (a_wo, j)] if (id(s[0]), s[1]) not in narrowed]
                wo = None if sides and sides[-1][0] is a_wo else weight(a_wo, j)
                v, results = _mixer_dilated(xf, xb, weight(a_wqkv, j), wo, alpha, B, S, D, narrow=sides)
                record(sides, results)
            elif kind == 2:
                v = _mixer_fox(xf, xb, weight(c_win, j), c_bf[j], weight(c_wo, j), alpha, B, S, D)
            else:
                v = _mixer_conv(xf, xb, weight(d_win, j), d_conv[j], weight(d_wout, j), alpha, B, S, D)
            xf, xb = _layer_norm(v, ln_g[i, 0], ln_b[i, 0])
        sides = [(mlp_w2, i)] + upcoming(i + 1)
        hid, results = _matmul(xb, *weight(mlp_w1, i), n_cols=mlp_w1.shape[2], out_dtype=BF16, act="relu2", narrow=sides)
        record(sides, results)
        v = _matmul_bf16w(hid, *weight(mlp_w2, i), xf, alpha)
        xf, xb = _layer_norm(v, ln_g[i, 1], ln_b[i, 1])
    return xf.reshape(B, S, D).astype(x.dtype)
```
